```python
import jax, jax.numpy as jnp
from jax import lax
import numpy as np

D_MODEL = 1024
BATCH = 16
SEQ = 2048
DEPTH = 1

GLA_HEADS = 4
GLA_KEY = D_MODEL // 2
GLA_VAL = D_MODEL
GLA_DK = GLA_KEY // GLA_HEADS
GLA_DV = GLA_VAL // GLA_HEADS
GLA_GATE_RANK = 16
GLA_GATE_NORM = 16.0
GLA_CHUNK = 64
GLA_NORM_EPS = 1e-5

RWKV_HEAD = 64
RWKV_DIM = D_MODEL
RWKV_HEADS = RWKV_DIM // RWKV_HEAD
DECAY_LORA = 64
AAA_LORA = 64
GATE_LORA = 128
RWKV_GN_EPS = 64e-5

D_FF = 4 * D_MODEL
NORM_EPS = 1e-6

GLA_SIZES = (GLA_KEY, GLA_KEY, GLA_VAL, GLA_VAL, GLA_GATE_RANK)
RWKV_SIZES = (RWKV_DIM, RWKV_DIM, RWKV_DIM, DECAY_LORA, AAA_LORA, GATE_LORA)
GATE_SIZES = (D_MODEL, D_MODEL)
GLA_WIDTH = 2 * GLA_KEY + 2 * GLA_VAL + GLA_GATE_RANK
RWKV_WIDTH = 3 * RWKV_DIM + DECAY_LORA + AAA_LORA + GATE_LORA
IN_WIDTH = GLA_WIDTH + RWKV_WIDTH + 2 * D_MODEL

kernel_name = "gla_rwkv7_gated_hybrid_block"


def _split(t, sizes):
    out, off = [], 0
    for s in sizes:
        out.append(t[..., off:off + s])
        off += s
    return out


def _rmsnorm(x, g):
    xf = x.astype(jnp.float32)
    y = xf * lax.rsqrt(jnp.mean(xf * xf, axis=-1, keepdims=True) + NORM_EPS)
    return (y * g.astype(jnp.float32)).astype(x.dtype)


def _token_shift(z):
    return jnp.pad(z[:, :-1], ((0, 0), (1, 0), (0, 0)))


def _gla_chunked(q, k, v, log_a):
    B, S, H, DK = q.shape
    DV = v.shape[-1]
    C = GLA_CHUNK
    N = S // C
    q = q.reshape(B, N, C, H, DK) * (DK ** -0.5)
    k = k.reshape(B, N, C, H, DK)
    v = v.reshape(B, N, C, H, DV)
    b = jnp.cumsum(log_a.reshape(B, N, C, H, DK), axis=2)
    b_mid = b[:, :, C // 2 - 1:C // 2]
    scores = jnp.einsum('bnthk,bnshk->bnhts', q * jnp.exp(b - b_mid), k * jnp.exp(b_mid - b))
    causal = jnp.tril(jnp.ones((C, C), dtype=bool))
    scores = jnp.where(causal, scores, 0.0)
    o_intra = jnp.einsum('bnhts,bnshv->bnthv', scores, v)
    b_last = b[:, :, -1:]
    q_dec = q * jnp.exp(b)
    k_dec = k * jnp.exp(b_last - b)
    chunk_decay = jnp.exp(b_last[:, :, 0])

    def step(state, inp):
        qd, kd, vc, dec = inp
        o = jnp.einsum('bthk,bhkv->bthv', qd, state)
        state = state * dec[..., None] + jnp.einsum('bthk,bthv->bhkv', kd, vc)
        return state, o

    state0 = jnp.zeros((B, H, DK, DV), jnp.float32)
    xs = (jnp.moveaxis(q_dec, 1, 0), jnp.moveaxis(k_dec, 1, 0), jnp.moveaxis(v, 1, 0), jnp.moveaxis(chunk_decay, 1, 0))
    _, o_inter = lax.scan(step, state0, xs)
    o = o_intra + jnp.moveaxis(o_inter, 0, 1)
    return o.reshape(B, S, H, DV)


def _rwkv7_scan(r, w, k, v, a_, b_):
    B, S, H, N = r.shape

    def step(state, inp):
        rt, wt, kt, vt, at, bt = inp
        sa = jnp.einsum('bhvk,bhk->bhv', state, at)
        state = state * wt[:, :, None, :] + sa[..., None] * bt[:, :, None, :] + vt[..., None] * kt[:, :, None, :]
        y = jnp.einsum('bhvk,bhk->bhv', state, rt)
        return state, y

    state0 = jnp.zeros((B, H, N, N), jnp.float32)
    xs = tuple(jnp.moveaxis(t, 1, 0) for t in (r, w, k, v, a_, b_))
    _, y = lax.scan(step, state0, xs)
    return jnp.moveaxis(y, 0, 1)


def setup_inputs(seed: int = 0) -> dict:
    key = jax.random.key(seed)
    ks = jax.random.split(key, 32)
    f32 = jnp.float32
    L, D = DEPTH, D_MODEL

    def nrm(k, shape, scale):
        return jax.random.normal(k, shape, f32) * scale

    return {
        "x": jax.random.normal(ks[0], (BATCH, SEQ, D), f32),
        "norm_mix": 1.0 + nrm(ks[1], (L, D), 0.05),
        "w_in": nrm(ks[2], (L, D, IN_WIDTH), D ** -0.5),
        "gla_a_w2": nrm(ks[3], (L, GLA_GATE_RANK, GLA_KEY), GLA_GATE_RANK ** -0.5),
        "gla_a_b": 0.5 + nrm(ks[4], (L, GLA_KEY), 0.1),
        "gla_norm": 1.0 + nrm(ks[5], (L, GLA_DV), 0.05),
        "rwkv_mu": jax.random.uniform(ks[6], (L, RWKV_WIDTH), f32, 0.0, 1.0),
        "rwkv_w0": -1.0 + nrm(ks[7], (L, RWKV_DIM), 0.5),
        "rwkv_w_w2": nrm(ks[8], (L, DECAY_LORA, RWKV_DIM), 0.5 * DECAY_LORA ** -0.5),
        "rwkv_a0": nrm(ks[9], (L, RWKV_DIM), 0.3),
        "rwkv_a_w2": nrm(ks[10], (L, AAA_LORA, RWKV_DIM), AAA_LORA ** -0.5),
        "rwkv_g_w2": nrm(ks[11], (L, GATE_LORA, RWKV_DIM), GATE_LORA ** -0.5),
        "rwkv_k_k": 0.85 + nrm(ks[12], (L, RWKV_DIM), 0.05),
        "rwkv_k_a": 1.0 + nrm(ks[13], (L, RWKV_DIM), 0.05),
        "rwkv_r_k": nrm(ks[14], (L, RWKV_HEADS, RWKV_HEAD), 0.1),
        "rwkv_ln_w": 1.0 + nrm(ks[15], (L, RWKV_DIM), 0.05),
        "rwkv_ln_b": nrm(ks[16], (L, RWKV_DIM), 0.02),
        "w_branch_gla": nrm(ks[17], (L, GLA_VAL, D), GLA_VAL ** -0.5),
        "w_branch_rwkv": nrm(ks[18], (L, RWKV_DIM, D), RWKV_DIM ** -0.5),
        "w_out": nrm(ks[19], (L, D, D), D ** -0.5),
        "norm_mlp": 1.0 + nrm(ks[20], (L, D), 0.05),
        "w_up": nrm(ks[21], (L, D, D_FF), D ** -0.5),
        "w_down": nrm(ks[22], (L, D_FF, D), D_FF ** -0.5),
        "norm_final": 1.0 + nrm(ks[23], (D,), 0.05),
    }


def reference(x, norm_mix, w_in, gla_a_w2, gla_a_b, gla_norm, rwkv_mu, rwkv_w0, rwkv_w_w2,
              rwkv_a0, rwkv_a_w2, rwkv_g_w2, rwkv_k_k, rwkv_k_a, rwkv_r_k, rwkv_ln_w, rwkv_ln_b,
              w_branch_gla, w_branch_rwkv, w_out, norm_mlp, w_up, w_down, norm_final):
    f32 = jnp.float32
    B, S, _ = x.shape
    for l in range(DEPTH):
        h = _rmsnorm(x, norm_mix[l])
        proj = h @ w_in[l]
        gla_p, rwkv_p, gate_p = _split(proj, (GLA_WIDTH, RWKV_WIDTH, 2 * D_MODEL))

        g_q, g_k, g_v, g_og, g_al = _split(gla_p.astype(f32), GLA_SIZES)
        log_a = jax.nn.log_sigmoid(g_al @ gla_a_w2[l].astype(f32) + gla_a_b[l].astype(f32)) / GLA_GATE_NORM
        o = _gla_chunked(g_q.reshape(B, S, GLA_HEADS, GLA_DK),
                         g_k.reshape(B, S, GLA_HEADS, GLA_DK),
                         g_v.reshape(B, S, GLA_HEADS, GLA_DV),
                         log_a.reshape(B, S, GLA_HEADS, GLA_DK))
        o = o * lax.rsqrt(jnp.mean(o * o, axis=-1, keepdims=True) + GLA_NORM_EPS) * gla_norm[l].astype(f32)
        y_gla = (o.reshape(B, S, GLA_VAL) * jax.nn.silu(g_og)).astype(x.dtype)

        z = rwkv_p.astype(f32)
        z = z + (_token_shift(z) - z) * rwkv_mu[l].astype(f32)
        r, k, v, wl, al, gl = _split(z, RWKV_SIZES)
        w = -jax.nn.softplus(-(rwkv_w0[l].astype(f32) + jnp.tanh(wl) @ rwkv_w_w2[l].astype(f32))) - 0.5
        decay = jnp.exp(-jnp.exp(w))
        a = jax.nn.sigmoid(rwkv_a0[l].astype(f32) + al @ rwkv_a_w2[l].astype(f32))
        gate = jax.nn.sigmoid(gl) @ rwkv_g_w2[l].astype(f32)
        hs = (B, S, RWKV_HEADS, RWKV_HEAD)
        kk = (k * rwkv_k_k[l].astype(f32)).reshape(hs)
        kk = kk / jnp.maximum(jnp.linalg.norm(kk, axis=-1, keepdims=True), 1e-12)
        k = k * (1.0 + (a - 1.0) * rwkv_k_a[l].astype(f32))
        r4, k4, v4 = r.reshape(hs), k.reshape(hs), v.reshape(hs)
        a4 = a.reshape(hs)
        y = _rwkv7_scan(r4, decay.reshape(hs), k4, v4, -kk, kk * a4)
        mu = jnp.mean(y, axis=-1, keepdims=True)
        var = jnp.mean(jnp.square(y - mu), axis=-1, keepdims=True)
        y = ((y - mu) * lax.rsqrt(var + RWKV_GN_EPS)).reshape(B, S, RWKV_DIM)
        y = y * rwkv_ln_w[l].astype(f32) + rwkv_ln_b[l].astype(f32)
        bonus = jnp.sum(r4 * k4 * rwkv_r_k[l].astype(f32), axis=-1, keepdims=True) * v4
        y_rwkv = ((y + bonus.reshape(B, S, RWKV_DIM)) * gate).astype(x.dtype)

        ga, gb = _split(gate_p, GATE_SIZES)
        mixed = jax.nn.sigmoid(ga) * (y_gla @ w_branch_gla[l]) + jax.nn.sigmoid(gb) * (y_rwkv @ w_branch_rwkv[l])
        x = x + (mixed @ w_out[l]).astype(x.dtype)

        h2 = _rmsnorm(x, norm_mlp[l])
        x = x + (jnp.square(jax.nn.relu(h2 @ w_up[l])) @ w_down[l]).astype(x.dtype)
    return _rmsnorm(x, norm_final)
```

```python
import functools

import jax
import jax.numpy as jnp
from jax import lax
from jax.experimental import pallas as pl
from jax.experimental.pallas import tpu as pltpu

F32 = jnp.float32
BF16 = jnp.bfloat16

D_MODEL = 1024
GLA_HEADS = 4
GLA_KEY = D_MODEL // 2
GLA_VAL = D_MODEL
GLA_DK = GLA_KEY // GLA_HEADS
GLA_DV = GLA_VAL // GLA_HEADS
GLA_GATE_RANK = 16
GLA_GATE_NORM = 16.0
GLA_NORM_EPS = 1e-5
RWKV_HEAD = 64
RWKV_DIM = D_MODEL
DECAY_LORA = 64
AAA_LORA = 64
GATE_LORA = 128
RWKV_GN_EPS = 64e-5
D_FF = 4 * D_MODEL
NORM_EPS = 1e-6

GLA_WIDTH = 2 * GLA_KEY + 2 * GLA_VAL + GLA_GATE_RANK
RWKV_WIDTH = 3 * RWKV_DIM + DECAY_LORA + AAA_LORA + GATE_LORA

LANES = 128
CHUNK = 64
COL_GATE = 0
COL_GLA_QK = 2048
COL_GLA_V = 3072
COL_GLA_OG = 4096
COL_R = 5120
COL_K = 6144
COL_V = 7168
COL_LORA = 8192
COL_GAL = 8448
PROJ_WIDTH = 8704
LORA_WIDTH = DECAY_LORA + AAA_LORA + GATE_LORA

VMEM_LIMIT = 56 * 1024 * 1024


def _mm(a, b):
    return jnp.dot(a.astype(BF16), b.astype(BF16), preferred_element_type=F32)


def _mm_nt(a, b):
    return lax.dot_general(a.astype(BF16), b.astype(BF16), (((1,), (1,)), ((), ())),
                           preferred_element_type=F32)


def _mm_tn(a, b):
    return lax.dot_general(a.astype(BF16), b.astype(BF16), (((0,), (0,)), ((), ())),
                           preferred_element_type=F32)


def _cumsum_rows(tri, x):
    return jnp.dot(tri, x, precision=lax.Precision.HIGHEST, preferred_element_type=F32)


def _softplus(x):
    return jnp.maximum(x, 0.0) + jnp.log(1.0 + jnp.exp(-jnp.abs(x)))


def _sigmoid(x):
    return 1.0 / (1.0 + jnp.exp(-x))


def _iota(shape, dim):
    return lax.broadcasted_iota(jnp.int32, shape, dim)


def _inproj_kernel(x_ref, g_ref, w_ref, o_ref, h_ref):
    @pl.when(pl.program_id(1) == 0)
    def _():
        x = x_ref[...]
        ms = jnp.mean(x * x, axis=-1, keepdims=True)
        h_ref[...] = (x * lax.rsqrt(ms + NORM_EPS) * g_ref[...]).astype(BF16)

    o_ref[...] = jnp.dot(h_ref[...], w_ref[...], preferred_element_type=F32)


def _inproj(x2, g, wp, *, tm, tn):
    t = x2.shape[0]
    return pl.pallas_call(
        _inproj_kernel,
        grid=(t // tm, PROJ_WIDTH // tn),
        in_specs=[
            pl.BlockSpec((tm, D_MODEL), lambda i, j: (i, 0)),
            pl.BlockSpec((1, D_MODEL), lambda i, j: (0, 0)),
            pl.BlockSpec((D_MODEL, tn), lambda i, j: (0, j)),
        ],
        out_specs=pl.BlockSpec((tm, tn), lambda i, j: (i, j)),
        out_shape=jax.ShapeDtypeStruct((t, PROJ_WIDTH), F32),
        scratch_shapes=[pltpu.VMEM((tm, D_MODEL), BF16)],
        compiler_params=pltpu.CompilerParams(
            dimension_semantics=("parallel", "arbitrary"), vmem_limit_bytes=VMEM_LIMIT),
        name="inproj",
    )(x2, g, wp)


def _gla_kernel(qk_ref, v_ref, og_ref, gal_ref, w2_ref, ab_ref, gn_ref, y_ref, st_ref, *, n_chunks):
    @pl.when(pl.program_id(1) == 0)
    def _():
        st_ref[...] = jnp.zeros_like(st_ref)

    row = _iota((CHUNK, CHUNK), 0)
    col = _iota((CHUNK, CHUNK), 1)
    causal = row >= col
    tri = causal.astype(F32)
    w2 = w2_ref[...]
    ab = ab_ref[...]
    gn = gn_ref[...]

    def chunk(c, carry):
        rows = pl.ds(pl.multiple_of(c * CHUNK, CHUNK), CHUNK)
        la = -_softplus(-(_mm(gal_ref[rows, :], w2) + ab)) * (1.0 / GLA_GATE_NORM)
        b = _cumsum_rows(tri, la)
        b_mid = b[CHUNK // 2 - 1:CHUNK // 2, :]
        b_last = b[CHUNK - 1:CHUNK, :]
        q = qk_ref[rows, 0:GLA_KEY] * (GLA_DK ** -0.5)
        k = qk_ref[rows, GLA_KEY:2 * GLA_KEY]
        qi = (q * jnp.exp(b - b_mid)).astype(BF16)
        ki = (k * jnp.exp(b_mid - b)).astype(BF16)
        qd = (q * jnp.exp(b)).astype(BF16)
        kd = (k * jnp.exp(b_last - b)).astype(BF16)
        dec = jnp.exp(b_last)
        for h in range(GLA_HEADS):
            ks = slice(h * GLA_DK, (h + 1) * GLA_DK)
            vs = slice(h * GLA_DV, (h + 1) * GLA_DV)
            s = jnp.where(causal, _mm_nt(qi[:, ks], ki[:, ks]), 0.0)
            v = v_ref[rows, vs].astype(BF16)
            st = st_ref[h]
            o = _mm(s, v) + _mm_nt(qd[:, ks], st)
            st_ref[h] = st * dec[:, ks] + _mm_tn(v, kd[:, ks])
            o = o * lax.rsqrt(jnp.mean(o * o, axis=-1, keepdims=True) + GLA_NORM_EPS) * gn
            og = og_ref[rows, vs]
            y_ref[rows, vs] = (o * (og * _sigmoid(og))).astype(BF16)
        return carry

    lax.fori_loop(0, n_chunks, chunk, 0)


def _gla(proj, w2p, ab, gn, *, batch, seq, lblk):
    t = batch * seq
    nsb = seq // lblk
    row_map = lambda cb: (lambda b, i: (b * nsb + i, cb))
    const = lambda b, i: (0, 0)
    return pl.pallas_call(
        functools.partial(_gla_kernel, n_chunks=lblk // CHUNK),
        grid=(batch, nsb),
        in_specs=[
            pl.BlockSpec((lblk, 2 * GLA_KEY), row_map(COL_GLA_QK // (2 * GLA_KEY))),
            pl.BlockSpec((lblk, GLA_VAL), row_map(COL_GLA_V // GLA_VAL)),
            pl.BlockSpec((lblk, GLA_VAL), row_map(COL_GLA_OG // GLA_VAL)),
            pl.BlockSpec((lblk, LANES), row_map(COL_GAL // LANES)),
            pl.BlockSpec((LANES, GLA_KEY), const),
            pl.BlockSpec((1, GLA_KEY), const),
            pl.BlockSpec((1, GLA_DV), const),
        ],
        out_specs=pl.BlockSpec((lblk, GLA_VAL), lambda b, i: (b * nsb + i, 0)),
        out_shape=jax.ShapeDtypeStruct((t, GLA_VAL), BF16),
        scratch_shapes=[pltpu.VMEM((GLA_HEADS, GLA_DV, GLA_DK), F32)],
        compiler_params=pltpu.CompilerParams(
            dimension_semantics=("parallel", "arbitrary"), vmem_limit_bytes=VMEM_LIMIT),
        name="gla",
    )(proj, proj, proj, proj, w2p, ab, gn)


def _unit_lower_inverse(n, masks):
    eye, blk16, off32, off64 = masks
    nd = jnp.where(blk16, n, 0.0)
    x = eye + nd
    m = _mm(nd, nd)
    x = x + _mm(x, m)
    m = _mm(m, m)
    x = x + _mm(x, m)
    m = _mm(m, m)
    x = x + _mm(x, m)
    x = x + _mm(_mm(x, jnp.where(off32, n, 0.0)), x)
    x = x + _mm(_mm(x, jnp.where(off64, n, 0.0)), x)
    return x


def _rwkv_kernel(r_ref, k_ref, v_ref, lo_ref, mu_r_ref, mu_k_ref, mu_v_ref, mu_lo_ref,
                 w0_ref, a0_ref, kk_ref, ka_ref, rk_ref, lnw_ref, lnb_ref,
                 ww2_ref, aw2_ref, gw2_ref, y_ref,
                 pr_ref, pk_ref, pv_ref, plo_ref, st_ref, *, n_chunks, n_pairs):
    @pl.when(pl.program_id(2) == 0)
    def _():
        pr_ref[...] = jnp.zeros_like(pr_ref)
        pk_ref[...] = jnp.zeros_like(pk_ref)
        pv_ref[...] = jnp.zeros_like(pv_ref)
        plo_ref[...] = jnp.zeros_like(plo_ref)
        st_ref[...] = jnp.zeros_like(st_ref)

    width = n_pairs * LANES
    row = _iota((CHUNK, CHUNK), 0)
    col = _iota((CHUNK, CHUNK), 1)
    tri = (row >= col).astype(F32)
    eye = (row == col).astype(F32)
    blk16 = (row // 16) == (col // 16)
    blk32 = (row // 32) == (col // 32)
    inv_masks = (eye, blk16, blk32 & (~blk16), ~blk32)

    prow = _iota((CHUNK, LANES), 0)
    plane = _iota((CHUNK, LANES), 1)
    lo = plane < RWKV_HEAD
    ptok = plane & (RWKV_HEAD - 1)
    strict = prow > ptok
    incl = prow >= ptok
    srow = _iota((LANES, LANES), 0)
    scol = _iota((LANES, LANES), 1)
    bdiag = (srow < RWKV_HEAD) == (scol < RWKV_HEAD)

    first_w = _iota((CHUNK, width), 0) == 0
    first_lo = _iota((CHUNK, LORA_WIDTH), 0) == 0

    def seg_sum(x):
        s0 = jnp.sum(jnp.where(lo, x, 0.0), axis=-1, keepdims=True)
        s1 = jnp.sum(jnp.where(lo, 0.0, x), axis=-1, keepdims=True)
        return jnp.where(lo, s0, s1)

    def shifted(z, prev_ref, first):
        zs = jnp.where(first, prev_ref[...], pltpu.roll(z, 1, axis=0))
        prev_ref[...] = z[CHUNK - 1:CHUNK, :]
        return zs

    def chunk(c, carry):
        rows = pl.ds(pl.multiple_of(c * CHUNK, CHUNK), CHUNK)
        zr = r_ref[rows, :]
        zk = k_ref[rows, :]
        zv = v_ref[rows, :]
        zl = lo_ref[rows, :]
        r_all = zr + (shifted(zr, pr_ref, first_w) - zr) * mu_r_ref[...]
        k_all = zk + (shifted(zk, pk_ref, first_w) - zk) * mu_k_ref[...]
        v_all = zv + (shifted(zv, pv_ref, first_w) - zv) * mu_v_ref[...]
        lora = zl + (shifted(zl, plo_ref, first_lo) - zl) * mu_lo_ref[...]
        wa = lora[:, 0:LANES]
        gl = lora[:, LANES:LORA_WIDTH]
        w_raw = -_softplus(-(w0_ref[...] + _mm(jnp.tanh(wa), ww2_ref[...]))) - 0.5
        logw_all = -jnp.exp(w_raw)
        a_all = _sigmoid(a0_ref[...] + _mm(wa, aw2_ref[...]))
        gate_all = _mm(_sigmoid(gl), gw2_ref[...])
        lw_all = _cumsum_rows(tri, logw_all)

        for p in range(n_pairs):
            ls = slice(p * LANES, (p + 1) * LANES)
            r, k, v = r_all[:, ls], k_all[:, ls], v_all[:, ls]
            a, logw, lw = a_all[:, ls], logw_all[:, ls], lw_all[:, ls]
            kk = k * kk_ref[:, ls]
            kkn = kk / jnp.maximum(jnp.sqrt(seg_sum(kk * kk)), 1e-12)
            kmod = k * (1.0 + (a - 1.0) * ka_ref[:, ls])
            avec = -kkn
            bvec = kkn * a
            lwx = lw - logw
            mid = lw[CHUNK // 2 - 1:CHUNK // 2, :]
            last = lw[CHUNK - 1:CHUNK, :]
            e_in = jnp.exp(lw - mid)
            e_ex = jnp.exp(lwx - mid)
            e_ng = jnp.exp(mid - lw)
            g_ls = jnp.exp(last - lw)
            ag = avec * e_ex
            rg = r * e_in
            l4 = jnp.concatenate([jnp.where(lo, ag, 0.0), jnp.where(lo, 0.0, ag),
                                  jnp.where(lo, rg, 0.0), jnp.where(lo, 0.0, rg)], axis=0)
            bk = jnp.concatenate([bvec * e_ng, kmod * e_ng], axis=0)
            pm = _mm_nt(l4, bk)
            pa0 = jnp.where(strict, pm[0:CHUNK], 0.0)
            pa1 = jnp.where(strict, pm[CHUNK:2 * CHUNK], 0.0)
            pr0 = jnp.where(incl, pm[2 * CHUNK:3 * CHUNK], 0.0)
            pr1 = jnp.where(incl, pm[3 * CHUNK:4 * CHUNK], 0.0)
            t0 = _unit_lower_inverse(pa0[:, 0:CHUNK], inv_masks)
            t1 = _unit_lower_inverse(pa1[:, 0:CHUNK], inv_masks)

            st = st_ref[p]
            ah = _mm_nt(jnp.concatenate([avec * jnp.exp(lwx), r * jnp.exp(lw)], axis=0), st)
            vv = jnp.concatenate([v, v], axis=0).astype(BF16)
            w = ah[0:CHUNK] + jnp.where(lo, _mm(jnp.where(lo, 0.0, pa0), vv),
                                        _mm(jnp.where(lo, 0.0, pa1), vv))
            u = jnp.where(lo, _mm(t0, w), _mm(t1, w))
            uv = jnp.concatenate([u, v], axis=0).astype(BF16)
            y = ah[CHUNK:2 * CHUNK] + jnp.where(lo, _mm(pr0, uv), _mm(pr1, uv))
            dh = _mm_tn(uv, jnp.concatenate([bvec * g_ls, kmod * g_ls], axis=0))
            st_ref[p] = st * jnp.exp(last) + jnp.where(bdiag, dh, 0.0)

            mean = seg_sum(y) * (1.0 / RWKV_HEAD)
            d = y - mean
            var = seg_sum(d * d) * (1.0 / RWKV_HEAD)
            yn = d * lax.rsqrt(var + RWKV_GN_EPS) * lnw_ref[:, ls] + lnb_ref[:, ls]
            bonus = seg_sum(r * kmod * rk_ref[:, ls]) * v
            y_ref[rows, ls] = ((yn + bonus) * gate_all[:, ls]).astype(BF16)
        return carry

    lax.fori_loop(0, n_chunks, chunk, 0)


def _rwkv(proj, params, *, batch, seq, lblk, n_pairs):
    (mu_r, mu_k, mu_v, mu_lo, w0, a0, k_k, k_a, r_k, ln_w, ln_b, ww2p, aw2p, gw2) = params
    t = batch * seq
    nsb = seq // lblk
    width = n_pairs * LANES
    ngrp = RWKV_DIM // width
    row_map = lambda c0: (lambda b, g, i: (b * nsb + i, c0 // width + g))
    grp = lambda b, g, i: (0, g)
    const = lambda b, g, i: (0, 0)
    vec = pl.BlockSpec((1, width), grp)
    return pl.pallas_call(
        functools.partial(_rwkv_kernel, n_chunks=lblk // CHUNK, n_pairs=n_pairs),
        grid=(batch, ngrp, nsb),
        in_specs=[
            pl.BlockSpec((lblk, width), row_map(COL_R)),
            pl.BlockSpec((lblk, width), row_map(COL_K)),
            pl.BlockSpec((lblk, width), row_map(COL_V)),
            pl.BlockSpec((lblk, LORA_WIDTH), lambda b, g, i: (b * nsb + i, COL_LORA // LORA_WIDTH)),
            vec, vec, vec,
            pl.BlockSpec((1, LORA_WIDTH), const),
            vec, vec, vec, vec, vec, vec, vec,
            pl.BlockSpec((LANES, width), grp),
            pl.BlockSpec((LANES, width), grp),
            pl.BlockSpec((GATE_LORA, width), grp),
        ],
        out_specs=pl.BlockSpec((lblk, width), lambda b, g, i: (b * nsb + i, g)),
        out_shape=jax.ShapeDtypeStruct((t, RWKV_DIM), BF16),
        scratch_shapes=[
            pltpu.VMEM((1, width), F32), pltpu.VMEM((1, width), F32), pltpu.VMEM((1, width), F32),
            pltpu.VMEM((1, LORA_WIDTH), F32),
            pltpu.VMEM((n_pairs, LANES, LANES), F32),
        ],
        compiler_params=pltpu.CompilerParams(
            dimension_semantics=("parallel", "parallel", "arbitrary"), vmem_limit_bytes=VMEM_LIMIT),
        name="rwkv",
    )(proj, proj, proj, proj, mu_r, mu_k, mu_v, mu_lo, w0, a0, k_k, k_a, r_k, ln_w, ln_b,
      ww2p, aw2p, gw2)


def _merge_kernel(x_ref, yg_ref, yr_ref, ga_ref, gb_ref, wg_ref, wr_ref, wo_ref, o_ref):
    mixed = (_sigmoid(ga_ref[...]) * jnp.dot(yg_ref[...], wg_ref[...], preferred_element_type=F32)
             + _sigmoid(gb_ref[...]) * jnp.dot(yr_ref[...], wr_ref[...], preferred_element_type=F32))
    o_ref[...] = x_ref[...] + _mm(mixed, wo_ref[...])


def _merge(x2, yg, yr, proj, wg, wr, wo, *, tm):
    t = x2.shape[0]
    tok = lambda i: (i, 0)
    const = lambda i: (0, 0)
    wspec = pl.BlockSpec((D_MODEL, D_MODEL), const)
    return pl.pallas_call(
        _merge_kernel,
        grid=(t // tm,),
        in_specs=[
            pl.BlockSpec((tm, D_MODEL), tok),
            pl.BlockSpec((tm, D_MODEL), tok),
            pl.BlockSpec((tm, D_MODEL), tok),
            pl.BlockSpec((tm, D_MODEL), lambda i: (i, COL_GATE // D_MODEL)),
            pl.BlockSpec((tm, D_MODEL), lambda i: (i, COL_GATE // D_MODEL + 1)),
            wspec, wspec, wspec,
        ],
        out_specs=pl.BlockSpec((tm, D_MODEL), tok),
        out_shape=jax.ShapeDtypeStruct((t, D_MODEL), F32),
        compiler_params=pltpu.CompilerParams(
            dimension_semantics=("parallel",), vmem_limit_bytes=VMEM_LIMIT),
        name="merge",
    )(x2, yg, yr, proj, proj, wg, wr, wo)


def _mlp_kernel(x_ref, g_ref, wu_ref, wd_ref, gf_ref, o_ref, *, final_norm):
    x = x_ref[...]
    h = (x * lax.rsqrt(jnp.mean(x * x, axis=-1, keepdims=True) + NORM_EPS) * g_ref[...]).astype(BF16)
    up = jnp.dot(h, wu_ref[...], preferred_element_type=F32)
    act = jnp.square(jnp.maximum(up, 0.0)).astype(BF16)
    x = x + jnp.dot(act, wd_ref[...], preferred_element_type=F32)
    if final_norm:
        x = x * lax.rsqrt(jnp.mean(x * x, axis=-1, keepdims=True) + NORM_EPS) * gf_ref[...]
    o_ref[...] = x


def _mlp(x2, g, wu, wd, gf, *, tm, final_norm):
    t = x2.shape[0]
    tok = lambda i: (i, 0)
    const = lambda i: (0, 0)
    return pl.pallas_call(
        functools.partial(_mlp_kernel, final_norm=final_norm),
        grid=(t // tm,),
        in_specs=[
            pl.BlockSpec((tm, D_MODEL), tok),
            pl.BlockSpec((1, D_MODEL), const),
            pl.BlockSpec((D_MODEL, D_FF), const),
            pl.BlockSpec((D_FF, D_MODEL), const),
            pl.BlockSpec((1, D_MODEL), const),
        ],
        out_specs=pl.BlockSpec((tm, D_MODEL), tok),
        out_shape=jax.ShapeDtypeStruct((t, D_MODEL), F32),
        compiler_params=pltpu.CompilerParams(
            dimension_semantics=("parallel",), vmem_limit_bytes=VMEM_LIMIT),
        name="mlp",
    )(x2, g, wu, wd, gf)


def _regroup_w_in(w):
    gla_w = w[:, :GLA_WIDTH]
    rw = w[:, GLA_WIDTH:GLA_WIDTH + RWKV_WIDTH]
    gates = w[:, GLA_WIDTH + RWKV_WIDTH:]
    gla_main = GLA_WIDTH - GLA_GATE_RANK
    pad = jnp.zeros((w.shape[0], PROJ_WIDTH - COL_GAL - GLA_GATE_RANK), w.dtype)
    return jnp.concatenate([gates, gla_w[:, :gla_main], rw, gla_w[:, gla_main:], pad], axis=1).astype(BF16)


def _row(v):
    return v.reshape(1, -1).astype(F32)


def kernel(x, norm_mix, w_in, gla_a_w2, gla_a_b, gla_norm, rwkv_mu, rwkv_w0, rwkv_w_w2, rwkv_a0, rwkv_a_w2, rwkv_g_w2, rwkv_k_k, rwkv_k_a, rwkv_r_k, rwkv_ln_w, rwkv_ln_b, w_branch_gla, w_branch_rwkv, w_out, norm_mlp, w_up, w_down, norm_final):
    batch, seq, d = x.shape
    assert d == D_MODEL and seq % CHUNK == 0
    t = batch * seq
    depth = norm_mix.shape[0]
    tm = 512 if t % 512 == 0 else CHUNK
    lblk = 512 if seq % 512 == 0 else CHUNK
    x2 = x.reshape(t, d)
    for l in range(depth):
        proj = _inproj(x2, _row(norm_mix[l]), _regroup_w_in(w_in[l]), tm=tm, tn=PROJ_WIDTH // 4)

        w2p = jnp.zeros((LANES, GLA_KEY), BF16).at[:GLA_GATE_RANK].set(gla_a_w2[l].astype(BF16))
        y_gla = _gla(proj, w2p, _row(gla_a_b[l]), _row(gla_norm[l]), batch=batch, seq=seq, lblk=lblk)

        mu = rwkv_mu[l]
        zeros = jnp.zeros((DECAY_LORA, RWKV_DIM), BF16)
        ww2p = jnp.concatenate([rwkv_w_w2[l].astype(BF16), zeros], axis=0)
        aw2p = jnp.concatenate([zeros, rwkv_a_w2[l].astype(BF16)], axis=0)
        params = (_row(mu[0:RWKV_DIM]), _row(mu[RWKV_DIM:2 * RWKV_DIM]), _row(mu[2 * RWKV_DIM:3 * RWKV_DIM]),
                  _row(mu[3 * RWKV_DIM:]), _row(rwkv_w0[l]), _row(rwkv_a0[l]), _row(rwkv_k_k[l]),
                  _row(rwkv_k_a[l]), _row(rwkv_r_k[l]), _row(rwkv_ln_w[l]), _row(rwkv_ln_b[l]),
                  ww2p, aw2p, rwkv_g_w2[l].astype(BF16))
        y_rwkv = _rwkv(proj, params, batch=batch, seq=seq, lblk=lblk, n_pairs=2)

        x2 = _merge(x2, y_gla, y_rwkv, proj, w_branch_gla[l].astype(BF16), w_branch_rwkv[l].astype(BF16),
                    w_out[l].astype(BF16), tm=tm)
        x2 = _mlp(x2, _row(norm_mlp[l]), w_up[l].astype(BF16), w_down[l].astype(BF16), _row(norm_final),
                  tm=tm, final_norm=(l == depth - 1))
    return x2.reshape(batch, seq, d)
```

```python
import functools

import jax
import jax.numpy as jnp
from jax import lax
from jax.experimental import pallas as pl
from jax.experimental.pallas import tpu as pltpu

F32 = jnp.float32
BF16 = jnp.bfloat16

D_MODEL = 1024
GLA_HEADS = 4
GLA_KEY = D_MODEL // 2
GLA_VAL = D_MODEL
GLA_DK = GLA_KEY // GLA_HEADS
GLA_DV = GLA_VAL // GLA_HEADS
GLA_GATE_RANK = 16
GLA_GATE_NORM = 16.0
GLA_NORM_EPS = 1e-5
RWKV_HEAD = 64
RWKV_DIM = D_MODEL
DECAY_LORA = 64
AAA_LORA = 64
GATE_LORA = 128
RWKV_GN_EPS = 64e-5
D_FF = 4 * D_MODEL
NORM_EPS = 1e-6

GLA_WIDTH = 2 * GLA_KEY + 2 * GLA_VAL + GLA_GATE_RANK
RWKV_WIDTH = 3 * RWKV_DIM + DECAY_LORA + AAA_LORA + GATE_LORA

LANES = 128
CHUNK = 64
COL_GATE = 0
COL_GLA_QK = 2048
COL_GLA_V = 3072
COL_GLA_OG = 4096
COL_R = 5120
COL_K = 6144
COL_V = 7168
COL_LORA = 8192
COL_GAL = 8448
PROJ_WIDTH = 8704
LORA_WIDTH = DECAY_LORA + AAA_LORA + GATE_LORA

VMEM_LIMIT = 56 * 1024 * 1024


def _mm(a, b):
    return jnp.dot(a.astype(BF16), b.astype(BF16), preferred_element_type=F32)


def _mm_nt(a, b):
    return lax.dot_general(a.astype(BF16), b.astype(BF16), (((1,), (1,)), ((), ())),
                           preferred_element_type=F32)


def _mm_tn(a, b):
    return lax.dot_general(a.astype(BF16), b.astype(BF16), (((0,), (0,)), ((), ())),
                           preferred_element_type=F32)


def _cumsum_rows(tri, x):
    return jnp.dot(tri, x, precision=lax.Precision.HIGHEST, preferred_element_type=F32)


def _softplus(x):
    return jnp.maximum(x, 0.0) + jnp.log(1.0 + jnp.exp(-jnp.abs(x)))


def _sigmoid(x):
    return 1.0 / (1.0 + jnp.exp(-x))


def _iota(shape, dim):
    return lax.broadcasted_iota(jnp.int32, shape, dim)


def _inproj_kernel(x_ref, g_ref, w_ref, o_ref, h_ref):
    @pl.when(pl.program_id(1) == 0)
    def _():
        x = x_ref[...]
        ms = jnp.mean(x * x, axis=-1, keepdims=True)
        h_ref[...] = (x * lax.rsqrt(ms + NORM_EPS) * g_ref[...]).astype(BF16)

    o_ref[...] = jnp.dot(h_ref[...], w_ref[...], preferred_element_type=F32)


def _inproj(x2, g, wp, *, tm, tn):
    t = x2.shape[0]
    return pl.pallas_call(
        _inproj_kernel,
        grid=(t // tm, PROJ_WIDTH // tn),
        in_specs=[
            pl.BlockSpec((tm, D_MODEL), lambda i, j: (i, 0)),
            pl.BlockSpec((1, D_MODEL), lambda i, j: (0, 0)),
            pl.BlockSpec((D_MODEL, tn), lambda i, j: (0, j)),
        ],
        out_specs=pl.BlockSpec((tm, tn), lambda i, j: (i, j)),
        out_shape=jax.ShapeDtypeStruct((t, PROJ_WIDTH), F32),
        scratch_shapes=[pltpu.VMEM((tm, D_MODEL), BF16)],
        compiler_params=pltpu.CompilerParams(
            dimension_semantics=("parallel", "arbitrary"), vmem_limit_bytes=VMEM_LIMIT),
        name="inproj",
    )(x2, g, wp)


def _gla_kernel(qk_ref, v_ref, og_ref, gal_ref, w2_ref, ab_ref, gn_ref, y_ref, st_ref, *, n_chunks):
    @pl.when(pl.program_id(1) == 0)
    def _():
        st_ref[...] = jnp.zeros_like(st_ref)

    row = _iota((CHUNK, CHUNK), 0)
    col = _iota((CHUNK, CHUNK), 1)
    causal = row >= col
    tri = causal.astype(F32)
    w2 = w2_ref[...]
    ab = ab_ref[...]
    gn = gn_ref[...]

    def chunk(c, carry):
        rows = pl.ds(pl.multiple_of(c * CHUNK, CHUNK), CHUNK)
        la = -_softplus(-(_mm(gal_ref[rows, :], w2) + ab)) * (1.0 / GLA_GATE_NORM)
        b = _cumsum_rows(tri, la)
        b_mid = b[CHUNK // 2 - 1:CHUNK // 2, :]
        b_last = b[CHUNK - 1:CHUNK, :]
        q = qk_ref[rows, 0:GLA_KEY] * (GLA_DK ** -0.5)
        k = qk_ref[rows, GLA_KEY:2 * GLA_KEY]
        qi = (q * jnp.exp(b - b_mid)).astype(BF16)
        ki = (k * jnp.exp(b_mid - b)).astype(BF16)
        qd = (q * jnp.exp(b)).astype(BF16)
        kd = (k * jnp.exp(b_last - b)).astype(BF16)
        dec = jnp.exp(b_last)
        heads = range(GLA_HEADS)
        ks = [slice(h * GLA_DK, (h + 1) * GLA_DK) for h in heads]
        vs = [slice(h * GLA_DV, (h + 1) * GLA_DV) for h in heads]
        v = [v_ref[rows, vs[h]].astype(BF16) for h in heads]
        st = [st_ref[h] for h in heads]
        s = [_mm_nt(qi[:, ks[h]], ki[:, ks[h]]) for h in heads]
        inter = [_mm_nt(qd[:, ks[h]], st[h]) for h in heads]
        upd = [_mm_tn(v[h], kd[:, ks[h]]) for h in heads]
        intra = [_mm(jnp.where(causal, s[h], 0.0), v[h]) for h in heads]
        for h in heads:
            st_ref[h] = st[h] * dec[:, ks[h]] + upd[h]
            o = intra[h] + inter[h]
            o = o * lax.rsqrt(jnp.mean(o * o, axis=-1, keepdims=True) + GLA_NORM_EPS) * gn
            og = og_ref[rows, vs[h]]
            y_ref[rows, vs[h]] = (o * (og * _sigmoid(og))).astype(BF16)
        return carry

    lax.fori_loop(0, n_chunks, chunk, 0)


def _gla(proj, w2p, ab, gn, *, batch, seq, lblk):
    t = batch * seq
    nsb = seq // lblk
    row_map = lambda cb: (lambda b, i: (b * nsb + i, cb))
    const = lambda b, i: (0, 0)
    return pl.pallas_call(
        functools.partial(_gla_kernel, n_chunks=lblk // CHUNK),
        grid=(batch, nsb),
        in_specs=[
            pl.BlockSpec((lblk, 2 * GLA_KEY), row_map(COL_GLA_QK // (2 * GLA_KEY))),
            pl.BlockSpec((lblk, GLA_VAL), row_map(COL_GLA_V // GLA_VAL)),
            pl.BlockSpec((lblk, GLA_VAL), row_map(COL_GLA_OG // GLA_VAL)),
            pl.BlockSpec((lblk, LANES), row_map(COL_GAL // LANES)),
            pl.BlockSpec((LANES, GLA_KEY), const),
            pl.BlockSpec((1, GLA_KEY), const),
            pl.BlockSpec((1, GLA_DV), const),
        ],
        out_specs=pl.BlockSpec((lblk, GLA_VAL), lambda b, i: (b * nsb + i, 0)),
        out_shape=jax.ShapeDtypeStruct((t, GLA_VAL), BF16),
        scratch_shapes=[pltpu.VMEM((GLA_HEADS, GLA_DV, GLA_DK), F32)],
        compiler_params=pltpu.CompilerParams(
            dimension_semantics=("parallel", "arbitrary"), vmem_limit_bytes=VMEM_LIMIT),
        name="gla",
    )(proj, proj, proj, proj, w2p, ab, gn)


def _unit_lower_inverse(ns, masks):
    eye, blk16, off32, off64 = masks
    nd = [jnp.where(blk16, n, 0.0) for n in ns]
    x = [eye + d for d in nd]
    m = [_mm(d, d) for d in nd]
    for step in range(3):
        xm = [_mm(xi, mi) for xi, mi in zip(x, m)]
        if step < 2:
            m = [_mm(mi, mi) for mi in m]
        x = [xi + xmi for xi, xmi in zip(x, xm)]
    for off in (off32, off64):
        xn = [_mm(xi, jnp.where(off, n, 0.0)) for xi, n in zip(x, ns)]
        xnx = [_mm(a, xi) for a, xi in zip(xn, x)]
        x = [xi + b for xi, b in zip(x, xnx)]
    return x


def _rwkv_kernel(r_ref, k_ref, v_ref, lo_ref, mu_r_ref, mu_k_ref, mu_v_ref, mu_lo_ref,
                 w0_ref, a0_ref, kk_ref, ka_ref, rk_ref, lnw_ref, lnb_ref,
                 ww2_ref, aw2_ref, gw2_ref, y_ref,
                 pr_ref, pk_ref, pv_ref, plo_ref, st_ref, *, n_chunks, n_pairs):
    @pl.when(pl.program_id(2) == 0)
    def _():
        pr_ref[...] = jnp.zeros_like(pr_ref)
        pk_ref[...] = jnp.zeros_like(pk_ref)
        pv_ref[...] = jnp.zeros_like(pv_ref)
        plo_ref[...] = jnp.zeros_like(plo_ref)
        st_ref[...] = jnp.zeros_like(st_ref)

    width = n_pairs * LANES
    tri = (_iota((CHUNK, CHUNK), 0) >= _iota((CHUNK, CHUNK), 1)).astype(F32)

    srow = _iota((LANES, LANES), 0)
    scol = _iota((LANES, LANES), 1)
    same = (srow < RWKV_HEAD) == (scol < RWKV_HEAD)
    rtok = srow & (RWKV_HEAD - 1)
    ctok = scol & (RWKV_HEAD - 1)
    strict_same = (rtok > ctok) & same
    strict_other = (rtok > ctok) & (~same)
    incl = rtok >= ctok
    blk16 = (srow // 16) == (scol // 16)
    blk32 = (srow // 32) == (scol // 32)
    inv_masks = ((srow == scol).astype(F32), blk16, blk32 & (~blk16), same & (~blk32))

    lo = _iota((CHUNK, LANES), 1) < RWKV_HEAD
    first_w = _iota((CHUNK, width), 0) == 0
    first_lo = _iota((CHUNK, LORA_WIDTH), 0) == 0
    pairs = range(n_pairs)
    ls = [slice(p * LANES, (p + 1) * LANES) for p in pairs]

    def seg_sum(x):
        s0 = jnp.sum(jnp.where(lo, x, 0.0), axis=-1, keepdims=True)
        s1 = jnp.sum(jnp.where(lo, 0.0, x), axis=-1, keepdims=True)
        return jnp.where(lo, s0, s1)

    def halves(x):
        return jnp.where(lo, x[0:CHUNK], x[CHUNK:2 * CHUNK])

    def stack(a, b):
        return jnp.concatenate([a, b], axis=0)

    def shifted(z, prev_ref, first):
        zs = jnp.where(first, prev_ref[...], pltpu.roll(z, 1, axis=0))
        prev_ref[...] = z[CHUNK - 1:CHUNK, :]
        return zs

    def chunk(c, carry):
        rows = pl.ds(pl.multiple_of(c * CHUNK, CHUNK), CHUNK)
        zr = r_ref[rows, :]
        zk = k_ref[rows, :]
        zv = v_ref[rows, :]
        zl = lo_ref[rows, :]
        r_all = zr + (shifted(zr, pr_ref, first_w) - zr) * mu_r_ref[...]
        k_all = zk + (shifted(zk, pk_ref, first_w) - zk) * mu_k_ref[...]
        v_all = zv + (shifted(zv, pv_ref, first_w) - zv) * mu_v_ref[...]
        lora = zl + (shifted(zl, plo_ref, first_lo) - zl) * mu_lo_ref[...]
        wa = lora[:, 0:LANES]
        gl = lora[:, LANES:LORA_WIDTH]
        w_raw = -_softplus(-(w0_ref[...] + _mm(jnp.tanh(wa), ww2_ref[...]))) - 0.5
        logw_all = -jnp.exp(w_raw)
        a_all = _sigmoid(a0_ref[...] + _mm(wa, aw2_ref[...]))
        gate_all = _mm(_sigmoid(gl), gw2_ref[...])
        lw_all = _cumsum_rows(tri, logw_all)

        r = [r_all[:, s] for s in ls]
        k = [k_all[:, s] for s in ls]
        v = [v_all[:, s] for s in ls]
        a = [a_all[:, s] for s in ls]
        lw = [lw_all[:, s] for s in ls]
        lwx = [lw[p] - logw_all[:, ls[p]] for p in pairs]
        kk = [k[p] * kk_ref[:, ls[p]] for p in pairs]
        kkn = [kk[p] / jnp.maximum(jnp.sqrt(seg_sum(kk[p] * kk[p])), 1e-12) for p in pairs]
        kmod = [k[p] * (1.0 + (a[p] - 1.0) * ka_ref[:, ls[p]]) for p in pairs]
        bvec = [kkn[p] * a[p] for p in pairs]
        mid = [x[CHUNK // 2 - 1:CHUNK // 2, :] for x in lw]
        last = [x[CHUNK - 1:CHUNK, :] for x in lw]
        ag = [-kkn[p] * jnp.exp(lwx[p] - mid[p]) for p in pairs]
        rg = [r[p] * jnp.exp(lw[p] - mid[p]) for p in pairs]
        e_ng = [jnp.exp(mid[p] - lw[p]) for p in pairs]
        bg = [bvec[p] * e_ng[p] for p in pairs]
        kg = [kmod[p] * e_ng[p] for p in pairs]
        pm0 = [_mm_nt(stack(jnp.where(lo, ag[p], 0.0), jnp.where(lo, rg[p], 0.0)), stack(bg[p], kg[p]))
               for p in pairs]
        pm1 = [_mm_nt(stack(jnp.where(lo, 0.0, ag[p]), jnp.where(lo, 0.0, rg[p])), stack(kg[p], bg[p]))
               for p in pairs]
        st = [st_ref[p] for p in pairs]
        ah = [_mm_nt(stack(-kkn[p] * jnp.exp(lwx[p]), r[p] * jnp.exp(lw[p])), st[p]) for p in pairs]
        pa = [stack(pm0[p][0:CHUNK], pm1[p][0:CHUNK]) for p in pairs]
        prm = [jnp.where(incl, stack(pm0[p][CHUNK:], pm1[p][CHUNK:]), 0.0) for p in pairs]
        akv = [_mm(jnp.where(strict_other, pa[p], 0.0), stack(v[p], v[p])) for p in pairs]
        tinv = _unit_lower_inverse([jnp.where(strict_same, pa[p], 0.0) for p in pairs], inv_masks)
        w = [ah[p][0:CHUNK] + halves(akv[p]) for p in pairs]
        u = [halves(_mm(tinv[p], stack(w[p], w[p]))) for p in pairs]
        ys = [_mm(prm[p], stack(jnp.where(lo, u[p], v[p]), jnp.where(lo, v[p], u[p]))) for p in pairs]
        g_ls = [jnp.exp(last[p] - lw[p]) for p in pairs]
        dh = [_mm_tn(stack(u[p], v[p]), stack(bvec[p] * g_ls[p], kmod[p] * g_ls[p])) for p in pairs]
        for p in pairs:
            st_ref[p] = st[p] * jnp.exp(last[p]) + jnp.where(same, dh[p], 0.0)
            y = ah[p][CHUNK:] + halves(ys[p])
            mean = seg_sum(y) * (1.0 / RWKV_HEAD)
            d = y - mean
            var = seg_sum(d * d) * (1.0 / RWKV_HEAD)
            yn = d * lax.rsqrt(var + RWKV_GN_EPS) * lnw_ref[:, ls[p]] + lnb_ref[:, ls[p]]
            bonus = seg_sum(r[p] * kmod[p] * rk_ref[:, ls[p]]) * v[p]
            y_ref[rows, ls[p]] = ((yn + bonus) * gate_all[:, ls[p]]).astype(BF16)
        return carry

    lax.fori_loop(0, n_chunks, chunk, 0)


def _rwkv(proj, params, *, batch, seq, lblk, n_pairs):
    (mu_r, mu_k, mu_v, mu_lo, w0, a0, k_k, k_a, r_k, ln_w, ln_b, ww2p, aw2p, gw2) = params
    t = batch * seq
    nsb = seq // lblk
    width = n_pairs * LANES
    ngrp = RWKV_DIM // width
    row_map = lambda c0: (lambda b, g, i: (b * nsb + i, c0 // width + g))
    grp = lambda b, g, i: (0, g)
    const = lambda b, g, i: (0, 0)
    vec = pl.BlockSpec((1, width), grp)
    return pl.pallas_call(
        functools.partial(_rwkv_kernel, n_chunks=lblk // CHUNK, n_pairs=n_pairs),
        grid=(batch, ngrp, nsb),
        in_specs=[
            pl.BlockSpec((lblk, width), row_map(COL_R)),
            pl.BlockSpec((lblk, width), row_map(COL_K)),
            pl.BlockSpec((lblk, width), row_map(COL_V)),
            pl.BlockSpec((lblk, LORA_WIDTH), lambda b, g, i: (b * nsb + i, COL_LORA // LORA_WIDTH)),
            vec, vec, vec,
            pl.BlockSpec((1, LORA_WIDTH), const),
            vec, vec, vec, vec, vec, vec, vec,
            pl.BlockSpec((LANES, width), grp),
            pl.BlockSpec((LANES, width), grp),
            pl.BlockSpec((GATE_LORA, width), grp),
        ],
        out_specs=pl.BlockSpec((lblk, width), lambda b, g, i: (b * nsb + i, g)),
        out_shape=jax.ShapeDtypeStruct((t, RWKV_DIM), BF16),
        scratch_shapes=[
            pltpu.VMEM((1, width), F32), pltpu.VMEM((1, width), F32), pltpu.VMEM((1, width), F32),
            pltpu.VMEM((1, LORA_WIDTH), F32),
            pltpu.VMEM((n_pairs, LANES, LANES), F32),
        ],
        compiler_params=pltpu.CompilerParams(
            dimension_semantics=("parallel", "parallel", "arbitrary"), vmem_limit_bytes=VMEM_LIMIT),
        name="rwkv",
    )(proj, proj, proj, proj, mu_r, mu_k, mu_v, mu_lo, w0, a0, k_k, k_a, r_k, ln_w, ln_b,
      ww2p, aw2p, gw2)


def _merge_kernel(x_ref, yg_ref, yr_ref, ga_ref, gb_ref, wg_ref, wr_ref, wo_ref, o_ref):
    mixed = (_sigmoid(ga_ref[...]) * jnp.dot(yg_ref[...], wg_ref[...], preferred_element_type=F32)
             + _sigmoid(gb_ref[...]) * jnp.dot(yr_ref[...], wr_ref[...], preferred_element_type=F32))
    o_ref[...] = x_ref[...] + _mm(mixed, wo_ref[...])


def _merge(x2, yg, yr, proj, wg, wr, wo, *, tm):
    t = x2.shape[0]
    tok = lambda i: (i, 0)
    const = lambda i: (0, 0)
    wspec = pl.BlockSpec((D_MODEL, D_MODEL), const)
    return pl.pallas_call(
        _merge_kernel,
        grid=(t // tm,),
        in_specs=[
            pl.BlockSpec((tm, D_MODEL), tok),
            pl.BlockSpec((tm, D_MODEL), tok),
            pl.BlockSpec((tm, D_MODEL), tok),
            pl.BlockSpec((tm, D_MODEL), lambda i: (i, COL_GATE // D_MODEL)),
            pl.BlockSpec((tm, D_MODEL), lambda i: (i, COL_GATE // D_MODEL + 1)),
            wspec, wspec, wspec,
        ],
        out_specs=pl.BlockSpec((tm, D_MODEL), tok),
        out_shape=jax.ShapeDtypeStruct((t, D_MODEL), F32),
        compiler_params=pltpu.CompilerParams(
            dimension_semantics=("parallel",), vmem_limit_bytes=VMEM_LIMIT),
        name="merge",
    )(x2, yg, yr, proj, proj, wg, wr, wo)


def _mlp_kernel(x_ref, g_ref, wu_ref, wd_ref, gf_ref, o_ref, *, final_norm):
    x = x_ref[...]
    h = (x * lax.rsqrt(jnp.mean(x * x, axis=-1, keepdims=True) + NORM_EPS) * g_ref[...]).astype(BF16)
    up = jnp.dot(h, wu_ref[...], preferred_element_type=F32)
    act = jnp.square(jnp.maximum(up, 0.0)).astype(BF16)
    x = x + jnp.dot(act, wd_ref[...], preferred_element_type=F32)
    if final_norm:
        x = x * lax.rsqrt(jnp.mean(x * x, axis=-1, keepdims=True) + NORM_EPS) * gf_ref[...]
    o_ref[...] = x


def _mlp(x2, g, wu, wd, gf, *, tm, final_norm):
    t = x2.shape[0]
    tok = lambda i: (i, 0)
    const = lambda i: (0, 0)
    return pl.pallas_call(
        functools.partial(_mlp_kernel, final_norm=final_norm),
        grid=(t // tm,),
        in_specs=[
            pl.BlockSpec((tm, D_MODEL), tok),
            pl.BlockSpec((1, D_MODEL), const),
            pl.BlockSpec((D_MODEL, D_FF), const),
            pl.BlockSpec((D_FF, D_MODEL), const),
            pl.BlockSpec((1, D_MODEL), const),
        ],
        out_specs=pl.BlockSpec((tm, D_MODEL), tok),
        out_shape=jax.ShapeDtypeStruct((t, D_MODEL), F32),
        compiler_params=pltpu.CompilerParams(
            dimension_semantics=("parallel",), vmem_limit_bytes=VMEM_LIMIT),
        name="mlp",
    )(x2, g, wu, wd, gf)


def _regroup_w_in(w):
    gla_w = w[:, :GLA_WIDTH]
    rw = w[:, GLA_WIDTH:GLA_WIDTH + RWKV_WIDTH]
    gates = w[:, GLA_WIDTH + RWKV_WIDTH:]
    gla_main = GLA_WIDTH - GLA_GATE_RANK
    pad = jnp.zeros((w.shape[0], PROJ_WIDTH - COL_GAL - GLA_GATE_RANK), w.dtype)
    return jnp.concatenate([gates, gla_w[:, :gla_main], rw, gla_w[:, gla_main:], pad], axis=1).astype(BF16)


def _row(v):
    return v.reshape(1, -1).astype(F32)


def kernel(x, norm_mix, w_in, gla_a_w2, gla_a_b, gla_norm, rwkv_mu, rwkv_w0, rwkv_w_w2, rwkv_a0, rwkv_a_w2, rwkv_g_w2, rwkv_k_k, rwkv_k_a, rwkv_r_k, rwkv_ln_w, rwkv_ln_b, w_branch_gla, w_branch_rwkv, w_out, norm_mlp, w_up, w_down, norm_final):
    batch, seq, d = x.shape
    assert d == D_MODEL and seq % CHUNK == 0
    t = batch * seq
    depth = norm_mix.shape[0]
    tm = 512 if t % 512 == 0 else CHUNK
    lblk = 512 if seq % 512 == 0 else CHUNK
    x2 = x.reshape(t, d)
    for l in range(depth):
        proj = _inproj(x2, _row(norm_mix[l]), _regroup_w_in(w_in[l]), tm=tm, tn=PROJ_WIDTH // 4)

        w2p = jnp.zeros((LANES, GLA_KEY), BF16).at[:GLA_GATE_RANK].set(gla_a_w2[l].astype(BF16))
        y_gla = _gla(proj, w2p, _row(gla_a_b[l]), _row(gla_norm[l]), batch=batch, seq=seq, lblk=lblk)

        mu = rwkv_mu[l]
        zeros = jnp.zeros((DECAY_LORA, RWKV_DIM), BF16)
        ww2p = jnp.concatenate([rwkv_w_w2[l].astype(BF16), zeros], axis=0)
        aw2p = jnp.concatenate([zeros, rwkv_a_w2[l].astype(BF16)], axis=0)
        params = (_row(mu[0:RWKV_DIM]), _row(mu[RWKV_DIM:2 * RWKV_DIM]), _row(mu[2 * RWKV_DIM:3 * RWKV_DIM]),
                  _row(mu[3 * RWKV_DIM:]), _row(rwkv_w0[l]), _row(rwkv_a0[l]), _row(rwkv_k_k[l]),
                  _row(rwkv_k_a[l]), _row(rwkv_r_k[l]), _row(rwkv_ln_w[l]), _row(rwkv_ln_b[l]),
                  ww2p, aw2p, rwkv_g_w2[l].astype(BF16))
        y_rwkv = _rwkv(proj, params, batch=batch, seq=seq, lblk=lblk, n_pairs=4)

        x2 = _merge(x2, y_gla, y_rwkv, proj, w_branch_gla[l].astype(BF16), w_branch_rwkv[l].astype(BF16),
                    w_out[l].astype(BF16), tm=tm)
        x2 = _mlp(x2, _row(norm_mlp[l]), w_up[l].astype(BF16), w_down[l].astype(BF16), _row(norm_final),
                  tm=tm, final_norm=(l == depth - 1))
    return x2.reshape(batch, seq, d)
```

```python
import functools

import jax
import jax.numpy as jnp
from jax import lax
from jax.experimental import pallas as pl
from jax.experimental.pallas import tpu as pltpu

F32 = jnp.float32
BF16 = jnp.bfloat16

D_MODEL = 1024
GLA_HEADS = 4
GLA_KEY = D_MODEL // 2
GLA_VAL = D_MODEL
GLA_DK = GLA_KEY // GLA_HEADS
GLA_DV = GLA_VAL // GLA_HEADS
GLA_GATE_RANK = 16
GLA_GATE_NORM = 16.0
GLA_NORM_EPS = 1e-5
RWKV_HEAD = 64
RWKV_DIM = D_MODEL
DECAY_LORA = 64
AAA_LORA = 64
GATE_LORA = 128
RWKV_GN_EPS = 64e-5
D_FF = 4 * D_MODEL
NORM_EPS = 1e-6

GLA_WIDTH = 2 * GLA_KEY + 2 * GLA_VAL + GLA_GATE_RANK
RWKV_WIDTH = 3 * RWKV_DIM + DECAY_LORA + AAA_LORA + GATE_LORA

LANES = 128
CHUNK = 64
COL_GATE = 0
COL_GLA_QK = 2048
COL_GLA_V = 3072
COL_GLA_OG = 4096
COL_R = 5120
COL_K = 6144
COL_V = 7168
COL_LORA = 8192
COL_GAL = 8448
PROJ_WIDTH = 8704
LORA_WIDTH = DECAY_LORA + AAA_LORA + GATE_LORA

VMEM_LIMIT = 56 * 1024 * 1024


def _mm(a, b):
    return jnp.dot(a.astype(BF16), b.astype(BF16), preferred_element_type=F32)


def _mm_nt(a, b):
    return lax.dot_general(a.astype(BF16), b.astype(BF16), (((1,), (1,)), ((), ())),
                           preferred_element_type=F32)


def _mm_tn(a, b):
    return lax.dot_general(a.astype(BF16), b.astype(BF16), (((0,), (0,)), ((), ())),
                           preferred_element_type=F32)


def _cumsum_rows(tri, x):
    tri = tri.astype(BF16)
    hi = x.astype(BF16)
    lo = (x - hi.astype(F32)).astype(BF16)
    return (jnp.dot(tri, hi, preferred_element_type=F32) + jnp.dot(tri, lo, preferred_element_type=F32))


def _softplus(x):
    return jnp.maximum(x, 0.0) + jnp.log(1.0 + jnp.exp(-jnp.abs(x)))


def _sigmoid(x):
    return 1.0 / (1.0 + jnp.exp(-x))


def _iota(shape, dim):
    return lax.broadcasted_iota(jnp.int32, shape, dim)


def _inproj_kernel(x_ref, g_ref, w_ref, o_ref, h_ref):
    @pl.when(pl.program_id(1) == 0)
    def _():
        x = x_ref[...]
        ms = jnp.mean(x * x, axis=-1, keepdims=True)
        h_ref[...] = (x * lax.rsqrt(ms + NORM_EPS) * g_ref[...]).astype(BF16)

    o_ref[...] = jnp.dot(h_ref[...], w_ref[...], preferred_element_type=F32)


def _inproj(x2, g, wp, *, tm, tn):
    t = x2.shape[0]
    return pl.pallas_call(
        _inproj_kernel,
        grid=(t // tm, PROJ_WIDTH // tn),
        in_specs=[
            pl.BlockSpec((tm, D_MODEL), lambda i, j: (i, 0)),
            pl.BlockSpec((1, D_MODEL), lambda i, j: (0, 0)),
            pl.BlockSpec((D_MODEL, tn), lambda i, j: (0, j)),
        ],
        out_specs=pl.BlockSpec((tm, tn), lambda i, j: (i, j)),
        out_shape=jax.ShapeDtypeStruct((t, PROJ_WIDTH), F32),
        scratch_shapes=[pltpu.VMEM((tm, D_MODEL), BF16)],
        compiler_params=pltpu.CompilerParams(
            dimension_semantics=("parallel", "arbitrary"), vmem_limit_bytes=VMEM_LIMIT),
        name="inproj",
    )(x2, g, wp)


def _gla_kernel(qk_ref, v_ref, og_ref, gal_ref, w2_ref, ab_ref, gn_ref, y_ref, st_ref, *, n_chunks):
    @pl.when(pl.program_id(1) == 0)
    def _():
        st_ref[...] = jnp.zeros_like(st_ref)

    row = _iota((CHUNK, CHUNK), 0)
    col = _iota((CHUNK, CHUNK), 1)
    causal = row >= col
    tri = causal.astype(F32)
    w2 = w2_ref[...]
    ab = ab_ref[...]
    gn = gn_ref[...]

    def chunk(c, carry):
        rows = pl.ds(pl.multiple_of(c * CHUNK, CHUNK), CHUNK)
        la = -_softplus(-(_mm(gal_ref[rows, :], w2) + ab)) * (1.0 / GLA_GATE_NORM)
        b = _cumsum_rows(tri, la)
        b_mid = b[CHUNK // 2 - 1:CHUNK // 2, :]
        b_last = b[CHUNK - 1:CHUNK, :]
        q = qk_ref[rows, 0:GLA_KEY] * (GLA_DK ** -0.5)
        k = qk_ref[rows, GLA_KEY:2 * GLA_KEY]
        qi = (q * jnp.exp(b - b_mid)).astype(BF16)
        ki = (k * jnp.exp(b_mid - b)).astype(BF16)
        qd = (q * jnp.exp(b)).astype(BF16)
        kd = (k * jnp.exp(b_last - b)).astype(BF16)
        dec = jnp.exp(b_last)
        heads = range(GLA_HEADS)
        ks = [slice(h * GLA_DK, (h + 1) * GLA_DK) for h in heads]
        vs = [slice(h * GLA_DV, (h + 1) * GLA_DV) for h in heads]
        v = [v_ref[rows, vs[h]].astype(BF16) for h in heads]
        st = [st_ref[h] for h in heads]
        s = [_mm_nt(qi[:, ks[h]], ki[:, ks[h]]) for h in heads]
        inter = [_mm_nt(qd[:, ks[h]], st[h]) for h in heads]
        upd = [_mm_tn(v[h], kd[:, ks[h]]) for h in heads]
        intra = [_mm(jnp.where(causal, s[h], 0.0), v[h]) for h in heads]
        for h in heads:
            st_ref[h] = st[h] * dec[:, ks[h]] + upd[h]
            o = intra[h] + inter[h]
            o = o * lax.rsqrt(jnp.mean(o * o, axis=-1, keepdims=True) + GLA_NORM_EPS) * gn
            og = og_ref[rows, vs[h]]
            y_ref[rows, vs[h]] = (o * (og * _sigmoid(og))).astype(BF16)
        return carry

    lax.fori_loop(0, n_chunks, chunk, 0)


def _gla(proj, w2p, ab, gn, *, batch, seq, lblk):
    t = batch * seq
    nsb = seq // lblk
    row_map = lambda cb: (lambda b, i: (b * nsb + i, cb))
    const = lambda b, i: (0, 0)
    return pl.pallas_call(
        functools.partial(_gla_kernel, n_chunks=lblk // CHUNK),
        grid=(batch, nsb),
        in_specs=[
            pl.BlockSpec((lblk, 2 * GLA_KEY), row_map(COL_GLA_QK // (2 * GLA_KEY))),
            pl.BlockSpec((lblk, GLA_VAL), row_map(COL_GLA_V // GLA_VAL)),
            pl.BlockSpec((lblk, GLA_VAL), row_map(COL_GLA_OG // GLA_VAL)),
            pl.BlockSpec((lblk, LANES), row_map(COL_GAL // LANES)),
            pl.BlockSpec((LANES, GLA_KEY), const),
            pl.BlockSpec((1, GLA_KEY), const),
            pl.BlockSpec((1, GLA_DV), const),
        ],
        out_specs=pl.BlockSpec((lblk, GLA_VAL), lambda b, i: (b * nsb + i, 0)),
        out_shape=jax.ShapeDtypeStruct((t, GLA_VAL), BF16),
        scratch_shapes=[pltpu.VMEM((GLA_HEADS, GLA_DV, GLA_DK), F32)],
        compiler_params=pltpu.CompilerParams(
            dimension_semantics=("parallel", "arbitrary"), vmem_limit_bytes=VMEM_LIMIT),
        name="gla",
    )(proj, proj, proj, proj, w2p, ab, gn)


def _unit_lower_inverse(ns, masks, after_stage=None):
    after_stage = after_stage or {}
    stage = [0]

    def stage_done():
        hook = after_stage.get(stage[0])
        if hook is not None:
            hook()
        stage[0] += 1

    eye, blk16, off32, off64 = masks
    nd = [jnp.where(blk16, n, 0.0) for n in ns]
    x = [eye + d for d in nd]
    m = [_mm(d, d) for d in nd]
    stage_done()
    for step in range(3):
        xm = [_mm(xi, mi) for xi, mi in zip(x, m)]
        if step < 2:
            m = [_mm(mi, mi) for mi in m]
        x = [xi + xmi for xi, xmi in zip(x, xm)]
        stage_done()
    for off in (off32, off64):
        xn = [_mm(xi, jnp.where(off, n, 0.0)) for xi, n in zip(x, ns)]
        stage_done()
        xnx = [_mm(a, xi) for a, xi in zip(xn, x)]
        x = [xi + b for xi, b in zip(x, xnx)]
        stage_done()
    return x


def _rwkv_kernel(r_ref, k_ref, v_ref, lo_ref, mu_r_ref, mu_k_ref, mu_v_ref, mu_lo_ref,
                 w0_ref, a0_ref, kk_ref, ka_ref, rk_ref, lnw_ref, lnb_ref,
                 ww2_ref, aw2_ref, gw2_ref, y_ref,
                 pr_ref, pk_ref, pv_ref, plo_ref, st_ref,
                 n_ref, prm_ref, ahl_ref, bk2_ref, akv_ref, v2_ref, bonv_ref, gate_ref, elast_ref,
                 *, n_chunks, n_pairs):
    @pl.when(pl.program_id(2) == 0)
    def _():
        pr_ref[...] = jnp.zeros_like(pr_ref)
        pk_ref[...] = jnp.zeros_like(pk_ref)
        pv_ref[...] = jnp.zeros_like(pv_ref)
        plo_ref[...] = jnp.zeros_like(plo_ref)
        st_ref[...] = jnp.zeros_like(st_ref)

    width = n_pairs * LANES
    tri = (_iota((CHUNK, CHUNK), 0) >= _iota((CHUNK, CHUNK), 1)).astype(F32)

    srow = _iota((LANES, LANES), 0)
    scol = _iota((LANES, LANES), 1)
    same = (srow < RWKV_HEAD) == (scol < RWKV_HEAD)
    rtok = srow & (RWKV_HEAD - 1)
    ctok = scol & (RWKV_HEAD - 1)
    strict_same = (rtok > ctok) & same
    strict_other = (rtok > ctok) & (~same)
    incl = rtok >= ctok
    blk16 = (srow // 16) == (scol // 16)
    blk32 = (srow // 32) == (scol // 32)
    inv_masks = ((srow == scol).astype(F32), blk16, blk32 & (~blk16), same & (~blk32))

    lo = _iota((CHUNK, LANES), 1) < RWKV_HEAD
    first_w = _iota((CHUNK, width), 0) == 0
    first_lo = _iota((CHUNK, LORA_WIDTH), 0) == 0
    pairs = range(n_pairs)
    ls = [slice(p * LANES, (p + 1) * LANES) for p in pairs]

    def seg_sum(x):
        s0 = jnp.sum(jnp.where(lo, x, 0.0), axis=-1, keepdims=True)
        s1 = jnp.sum(jnp.where(lo, 0.0, x), axis=-1, keepdims=True)
        return jnp.where(lo, s0, s1)

    def halves(x):
        return jnp.where(lo, x[0:CHUNK], x[CHUNK:2 * CHUNK])

    def stack(a, b):
        return jnp.concatenate([a, b], axis=0)

    def shifted(z, prev_ref, first):
        zs = jnp.where(first, prev_ref[...], pltpu.roll(z, 1, axis=0))
        prev_ref[...] = z[CHUNK - 1:CHUNK, :]
        return zs

    def rows_of(c):
        if isinstance(c, int):
            return pl.ds(c * CHUNK, CHUNK)
        return pl.ds(pl.multiple_of(c * CHUNK, CHUNK), CHUNK)

    def stage1_lora(c):
        rows = rows_of(c)
        zl = lo_ref[rows, :]
        lora = zl + (shifted(zl, plo_ref, first_lo) - zl) * mu_lo_ref[...]
        wa = lora[:, 0:LANES]
        gl = lora[:, LANES:LORA_WIDTH]
        return (c, _mm(jnp.tanh(wa), ww2_ref[...]), _mm(wa, aw2_ref[...]),
                _mm(_sigmoid(gl), gw2_ref[...]))

    def stage1_decay(lora_out):
        c, w_lin, a_lin, gate_all = lora_out
        w_raw = -_softplus(-(w0_ref[...] + w_lin)) - 0.5
        logw_all = -jnp.exp(w_raw)
        return c, logw_all, a_lin, gate_all, _cumsum_rows(tri, logw_all)

    def stage1_front(c):
        return stage1_decay(stage1_lora(c))

    def stage1_back(front, slot):
        c, logw_all, a_lin, gate_all, lw_all = front
        rows = rows_of(c)
        zr = r_ref[rows, :]
        zk = k_ref[rows, :]
        zv = v_ref[rows, :]
        r_all = zr + (shifted(zr, pr_ref, first_w) - zr) * mu_r_ref[...]
        k_all = zk + (shifted(zk, pk_ref, first_w) - zk) * mu_k_ref[...]
        v_all = zv + (shifted(zv, pv_ref, first_w) - zv) * mu_v_ref[...]
        a_all = _sigmoid(a0_ref[...] + a_lin)

        r = [r_all[:, s] for s in ls]
        k = [k_all[:, s] for s in ls]
        v = [v_all[:, s] for s in ls]
        a = [a_all[:, s] for s in ls]
        lw = [lw_all[:, s] for s in ls]
        lwx = [lw[p] - logw_all[:, ls[p]] for p in pairs]
        kk = [k[p] * kk_ref[:, ls[p]] for p in pairs]
        kkn = [kk[p] / jnp.maximum(jnp.sqrt(seg_sum(kk[p] * kk[p])), 1e-12) for p in pairs]
        kmod = [k[p] * (1.0 + (a[p] - 1.0) * ka_ref[:, ls[p]]) for p in pairs]
        bvec = [kkn[p] * a[p] for p in pairs]
        mid = [x[CHUNK // 2 - 1:CHUNK // 2, :] for x in lw]
        last = [x[CHUNK - 1:CHUNK, :] for x in lw]
        ag = [-kkn[p] * jnp.exp(lwx[p] - mid[p]) for p in pairs]
        rg = [r[p] * jnp.exp(lw[p] - mid[p]) for p in pairs]
        e_ng = [jnp.exp(mid[p] - lw[p]) for p in pairs]
        bg = [bvec[p] * e_ng[p] for p in pairs]
        kg = [kmod[p] * e_ng[p] for p in pairs]
        pm0 = [_mm_nt(stack(jnp.where(lo, ag[p], 0.0), jnp.where(lo, rg[p], 0.0)), stack(bg[p], kg[p]))
               for p in pairs]
        pm1 = [_mm_nt(stack(jnp.where(lo, 0.0, ag[p]), jnp.where(lo, 0.0, rg[p])), stack(kg[p], bg[p]))
               for p in pairs]
        pa = [stack(pm0[p][0:CHUNK], pm1[p][0:CHUNK]) for p in pairs]
        akv = [_mm(jnp.where(strict_other, pa[p], 0.0), stack(v[p], v[p])) for p in pairs]
        for p in pairs:
            g_ls = jnp.exp(last[p] - lw[p])
            n_ref[slot, p] = jnp.where(strict_same, pa[p], 0.0)
            prm_ref[slot, p] = jnp.where(incl, stack(pm0[p][CHUNK:], pm1[p][CHUNK:]), 0.0).astype(BF16)
            akv_ref[slot, p] = halves(akv[p])
            ahl_ref[slot, p] = stack(-kkn[p] * jnp.exp(lwx[p]), r[p] * jnp.exp(lw[p])).astype(BF16)
            bk2_ref[slot, p] = stack(bvec[p] * g_ls, kmod[p] * g_ls).astype(BF16)
            v2_ref[slot, p] = v[p]
            bonv_ref[slot, p] = seg_sum(r[p] * kmod[p] * rk_ref[:, ls[p]]) * v[p]
            gate_ref[slot, p] = gate_all[:, ls[p]]
            elast_ref[slot, p] = jnp.exp(last[p])

    def stage2_front(slot, after_stage=None):
        st = [st_ref[p] for p in pairs]
        ah = [_mm_nt(ahl_ref[slot, p], st[p]) for p in pairs]
        tinv = _unit_lower_inverse([n_ref[slot, p] for p in pairs], inv_masks, after_stage)
        return st, ah, tinv

    def stage2_back(c, slot, front):
        st, ah, tinv = front
        rows = rows_of(c)
        w = [ah[p][0:CHUNK] + akv_ref[slot, p] for p in pairs]
        u = [halves(_mm(tinv[p], stack(w[p], w[p]))) for p in pairs]
        v = [v2_ref[slot, p] for p in pairs]
        ys = [_mm(prm_ref[slot, p], stack(jnp.where(lo, u[p], v[p]), jnp.where(lo, v[p], u[p])))
              for p in pairs]
        dh = [_mm_tn(stack(u[p], v[p]), bk2_ref[slot, p]) for p in pairs]
        for p in pairs:
            st_ref[p] = st[p] * elast_ref[slot, p] + jnp.where(same, dh[p], 0.0)
            y = ah[p][CHUNK:] + halves(ys[p])
            mean = seg_sum(y) * (1.0 / RWKV_HEAD)
            d = y - mean
            var = seg_sum(d * d) * (1.0 / RWKV_HEAD)
            yn = d * lax.rsqrt(var + RWKV_GN_EPS) * lnw_ref[:, ls[p]] + lnb_ref[:, ls[p]]
            y_ref[rows, ls[p]] = ((yn + bonv_ref[slot, p]) * gate_ref[slot, p]).astype(BF16)

    stage1_back(stage1_front(0), 0)

    def body(c, carry):
        slot = c & 1
        nxt = {}

        def emit_lora():
            nxt["lora"] = stage1_lora(c + 1)

        def emit_decay():
            nxt["front"] = stage1_decay(nxt["lora"])

        cur = stage2_front(slot, {0: emit_lora, 2: emit_decay})
        stage2_back(c, slot, cur)
        stage1_back(nxt["front"], 1 - slot)
        return carry

    lax.fori_loop(0, n_chunks - 1, body, 0)
    last_slot = (n_chunks - 1) & 1
    stage2_back(n_chunks - 1, last_slot, stage2_front(last_slot))


def _rwkv(proj, params, *, batch, seq, lblk, n_pairs):
    (mu_r, mu_k, mu_v, mu_lo, w0, a0, k_k, k_a, r_k, ln_w, ln_b, ww2p, aw2p, gw2) = params
    t = batch * seq
    nsb = seq // lblk
    width = n_pairs * LANES
    ngrp = RWKV_DIM // width
    row_map = lambda c0: (lambda b, g, i: (b * nsb + i, c0 // width + g))
    grp = lambda b, g, i: (0, g)
    const = lambda b, g, i: (0, 0)
    vec = pl.BlockSpec((1, width), grp)
    return pl.pallas_call(
        functools.partial(_rwkv_kernel, n_chunks=lblk // CHUNK, n_pairs=n_pairs),
        grid=(batch, ngrp, nsb),
        in_specs=[
            pl.BlockSpec((lblk, width), row_map(COL_R)),
            pl.BlockSpec((lblk, width), row_map(COL_K)),
            pl.BlockSpec((lblk, width), row_map(COL_V)),
            pl.BlockSpec((lblk, LORA_WIDTH), lambda b, g, i: (b * nsb + i, COL_LORA // LORA_WIDTH)),
            vec, vec, vec,
            pl.BlockSpec((1, LORA_WIDTH), const),
            vec, vec, vec, vec, vec, vec, vec,
            pl.BlockSpec((LANES, width), grp),
            pl.BlockSpec((LANES, width), grp),
            pl.BlockSpec((GATE_LORA, width), grp),
        ],
        out_specs=pl.BlockSpec((lblk, width), lambda b, g, i: (b * nsb + i, g)),
        out_shape=jax.ShapeDtypeStruct((t, RWKV_DIM), BF16),
        scratch_shapes=[
            pltpu.VMEM((1, width), F32), pltpu.VMEM((1, width), F32), pltpu.VMEM((1, width), F32),
            pltpu.VMEM((1, LORA_WIDTH), F32),
            pltpu.VMEM((n_pairs, LANES, LANES), F32),
            pltpu.VMEM((2, n_pairs, LANES, LANES), F32),
            pltpu.VMEM((2, n_pairs, LANES, LANES), BF16),
            pltpu.VMEM((2, n_pairs, LANES, LANES), BF16),
            pltpu.VMEM((2, n_pairs, LANES, LANES), BF16),
            pltpu.VMEM((2, n_pairs, CHUNK, LANES), F32),
            pltpu.VMEM((2, n_pairs, CHUNK, LANES), F32),
            pltpu.VMEM((2, n_pairs, CHUNK, LANES), F32),
            pltpu.VMEM((2, n_pairs, CHUNK, LANES), F32),
            pltpu.VMEM((2, n_pairs, 1, LANES), F32),
        ],
        compiler_params=pltpu.CompilerParams(
            dimension_semantics=("parallel", "parallel", "arbitrary"), vmem_limit_bytes=VMEM_LIMIT),
        name="rwkv",
    )(proj, proj, proj, proj, mu_r, mu_k, mu_v, mu_lo, w0, a0, k_k, k_a, r_k, ln_w, ln_b,
      ww2p, aw2p, gw2)


def _merge_kernel(x_ref, yg_ref, yr_ref, ga_ref, gb_ref, wg_ref, wr_ref, wo_ref, o_ref):
    mixed = (_sigmoid(ga_ref[...]) * jnp.dot(yg_ref[...], wg_ref[...], preferred_element_type=F32)
             + _sigmoid(gb_ref[...]) * jnp.dot(yr_ref[...], wr_ref[...], preferred_element_type=F32))
    o_ref[...] = x_ref[...] + _mm(mixed, wo_ref[...])


def _merge(x2, yg, yr, proj, wg, wr, wo, *, tm):
    t = x2.shape[0]
    tok = lambda i: (i, 0)
    const = lambda i: (0, 0)
    wspec = pl.BlockSpec((D_MODEL, D_MODEL), const)
    return pl.pallas_call(
        _merge_kernel,
        grid=(t // tm,),
        in_specs=[
            pl.BlockSpec((tm, D_MODEL), tok),
            pl.BlockSpec((tm, D_MODEL), tok),
            pl.BlockSpec((tm, D_MODEL), tok),
            pl.BlockSpec((tm, D_MODEL), lambda i: (i, COL_GATE // D_MODEL)),
            pl.BlockSpec((tm, D_MODEL), lambda i: (i, COL_GATE // D_MODEL + 1)),
            wspec, wspec, wspec,
        ],
        out_specs=pl.BlockSpec((tm, D_MODEL), tok),
        out_shape=jax.ShapeDtypeStruct((t, D_MODEL), F32),
        compiler_params=pltpu.CompilerParams(
            dimension_semantics=("parallel",), vmem_limit_bytes=VMEM_LIMIT),
        name="merge",
    )(x2, yg, yr, proj, proj, wg, wr, wo)


def _mlp_kernel(x_ref, g_ref, wu_ref, wd_ref, gf_ref, o_ref, *, final_norm):
    x = x_ref[...]
    h = (x * lax.rsqrt(jnp.mean(x * x, axis=-1, keepdims=True) + NORM_EPS) * g_ref[...]).astype(BF16)
    up = jnp.dot(h, wu_ref[...], preferred_element_type=F32)
    act = jnp.square(jnp.maximum(up, 0.0)).astype(BF16)
    x = x + jnp.dot(act, wd_ref[...], preferred_element_type=F32)
    if final_norm:
        x = x * lax.rsqrt(jnp.mean(x * x, axis=-1, keepdims=True) + NORM_EPS) * gf_ref[...]
    o_ref[...] = x


def _mlp(x2, g, wu, wd, gf, *, tm, final_norm):
    t = x2.shape[0]
    tok = lambda i: (i, 0)
    const = lambda i: (0, 0)
    return pl.pallas_call(
        functools.partial(_mlp_kernel, final_norm=final_norm),
        grid=(t // tm,),
        in_specs=[
            pl.BlockSpec((tm, D_MODEL), tok),
            pl.BlockSpec((1, D_MODEL), const),
            pl.BlockSpec((D_MODEL, D_FF), const),
            pl.BlockSpec((D_FF, D_MODEL), const),
            pl.BlockSpec((1, D_MODEL), const),
        ],
        out_specs=pl.BlockSpec((tm, D_MODEL), tok),
        out_shape=jax.ShapeDtypeStruct((t, D_MODEL), F32),
        compiler_params=pltpu.CompilerParams(
            dimension_semantics=("parallel",), vmem_limit_bytes=VMEM_LIMIT),
        name="mlp",
    )(x2, g, wu, wd, gf)


def _regroup_w_in(w):
    gla_w = w[:, :GLA_WIDTH]
    rw = w[:, GLA_WIDTH:GLA_WIDTH + RWKV_WIDTH]
    gates = w[:, GLA_WIDTH + RWKV_WIDTH:]
    gla_main = GLA_WIDTH - GLA_GATE_RANK
    pad = jnp.zeros((w.shape[0], PROJ_WIDTH - COL_GAL - GLA_GATE_RANK), w.dtype)
    return jnp.concatenate([gates, gla_w[:, :gla_main], rw, gla_w[:, gla_main:], pad], axis=1).astype(BF16)


def _row(v):
    return v.reshape(1, -1).astype(F32)


def kernel(x, norm_mix, w_in, gla_a_w2, gla_a_b, gla_norm, rwkv_mu, rwkv_w0, rwkv_w_w2, rwkv_a0, rwkv_a_w2, rwkv_g_w2, rwkv_k_k, rwkv_k_a, rwkv_r_k, rwkv_ln_w, rwkv_ln_b, w_branch_gla, w_branch_rwkv, w_out, norm_mlp, w_up, w_down, norm_final):
    batch, seq, d = x.shape
    assert d == D_MODEL and seq % CHUNK == 0
    t = batch * seq
    depth = norm_mix.shape[0]
    tm = 512 if t % 512 == 0 else CHUNK
    lblk = 512 if seq % 512 == 0 else CHUNK
    lblk_rwkv = 1024 if seq % 1024 == 0 else lblk
    x2 = x.reshape(t, d)
    for l in range(depth):
        proj = _inproj(x2, _row(norm_mix[l]), _regroup_w_in(w_in[l]), tm=tm, tn=PROJ_WIDTH // 4)

        w2p = jnp.zeros((LANES, GLA_KEY), BF16).at[:GLA_GATE_RANK].set(gla_a_w2[l].astype(BF16))
        y_gla = _gla(proj, w2p, _row(gla_a_b[l]), _row(gla_norm[l]), batch=batch, seq=seq, lblk=lblk)

        mu = rwkv_mu[l]
        zeros = jnp.zeros((DECAY_LORA, RWKV_DIM), BF16)
        ww2p = jnp.concatenate([rwkv_w_w2[l].astype(BF16), zeros], axis=0)
        aw2p = jnp.concatenate([zeros, rwkv_a_w2[l].astype(BF16)], axis=0)
        params = (_row(mu[0:RWKV_DIM]), _row(mu[RWKV_DIM:2 * RWKV_DIM]), _row(mu[2 * RWKV_DIM:3 * RWKV_DIM]),
                  _row(mu[3 * RWKV_DIM:]), _row(rwkv_w0[l]), _row(rwkv_a0[l]), _row(rwkv_k_k[l]),
                  _row(rwkv_k_a[l]), _row(rwkv_r_k[l]), _row(rwkv_ln_w[l]), _row(rwkv_ln_b[l]),
                  ww2p, aw2p, rwkv_g_w2[l].astype(BF16))
        y_rwkv = _rwkv(proj, params, batch=batch, seq=seq, lblk=lblk_rwkv, n_pairs=8)

        x2 = _merge(x2, y_gla, y_rwkv, proj, w_branch_gla[l].astype(BF16), w_branch_rwkv[l].astype(BF16),
                    w_out[l].astype(BF16), tm=tm)
        x2 = _mlp(x2, _row(norm_mlp[l]), w_up[l].astype(BF16), w_down[l].astype(BF16), _row(norm_final),
                  tm=tm, final_norm=(l == depth - 1))
    return x2.reshape(batch, seq, d)
```

```python
import functools

import jax
import jax.numpy as jnp
from jax import lax
from jax.experimental import pallas as pl
from jax.experimental.pallas import tpu as pltpu

F32 = jnp.float32
BF16 = jnp.bfloat16

D_MODEL = 1024
GLA_HEADS = 4
GLA_KEY = D_MODEL // 2
GLA_VAL = D_MODEL
GLA_DK = GLA_KEY // GLA_HEADS
GLA_DV = GLA_VAL // GLA_HEADS
GLA_GATE_RANK = 16
GLA_GATE_NORM = 16.0
GLA_NORM_EPS = 1e-5
RWKV_HEAD = 64
RWKV_DIM = D_MODEL
DECAY_LORA = 64
AAA_LORA = 64
GATE_LORA = 128
RWKV_GN_EPS = 64e-5
D_FF = 4 * D_MODEL
NORM_EPS = 1e-6

GLA_WIDTH = 2 * GLA_KEY + 2 * GLA_VAL + GLA_GATE_RANK
RWKV_WIDTH = 3 * RWKV_DIM + DECAY_LORA + AAA_LORA + GATE_LORA

LANES = 128
CHUNK = 64
COL_GATE = 0
COL_GLA_QK = 2048
COL_GLA_V = 3072
COL_GLA_OG = 4096
COL_R = 5120
COL_K = 6144
COL_V = 7168
COL_LORA = 8192
COL_GAL = 8448
PROJ_WIDTH = 8704
LORA_WIDTH = DECAY_LORA + AAA_LORA + GATE_LORA

VMEM_LIMIT = 56 * 1024 * 1024


def _mm(a, b):
    return jnp.dot(a.astype(BF16), b.astype(BF16), preferred_element_type=F32)


def _mm_nt(a, b):
    return lax.dot_general(a.astype(BF16), b.astype(BF16), (((1,), (1,)), ((), ())),
                           preferred_element_type=F32)


def _mm_tn(a, b):
    return lax.dot_general(a.astype(BF16), b.astype(BF16), (((0,), (0,)), ((), ())),
                           preferred_element_type=F32)


def _cumsum_rows(tri, x):
    tri = tri.astype(BF16)
    hi = x.astype(BF16)
    lo = (x - hi.astype(F32)).astype(BF16)
    return (jnp.dot(tri, hi, preferred_element_type=F32) + jnp.dot(tri, lo, preferred_element_type=F32))


def _softplus(x):
    return jnp.maximum(x, 0.0) + jnp.log(1.0 + jnp.exp(-jnp.abs(x)))


def _sigmoid(x):
    return 1.0 / (1.0 + jnp.exp(-x))


def _iota(shape, dim):
    return lax.broadcasted_iota(jnp.int32, shape, dim)


def _inproj_kernel(x_ref, g_ref, w_ref, o_ref, h_ref):
    @pl.when(pl.program_id(1) == 0)
    def _():
        x = x_ref[...]
        ms = jnp.mean(x * x, axis=-1, keepdims=True)
        h_ref[...] = (x * lax.rsqrt(ms + NORM_EPS) * g_ref[...]).astype(BF16)

    o_ref[...] = jnp.dot(h_ref[...], w_ref[pl.program_id(1)], preferred_element_type=F32)


def _inproj(x2, g, wp, *, tm, tn):
    t = x2.shape[0]
    return pl.pallas_call(
        _inproj_kernel,
        grid=(t // tm, PROJ_WIDTH // tn),
        in_specs=[
            pl.BlockSpec((tm, D_MODEL), lambda i, j: (i, 0)),
            pl.BlockSpec((1, D_MODEL), lambda i, j: (0, 0)),
            pl.BlockSpec((PROJ_WIDTH // tn, D_MODEL, tn), lambda i, j: (0, 0, 0),
                         pipeline_mode=pl.Buffered(1)),
        ],
        out_specs=pl.BlockSpec((tm, tn), lambda i, j: (i, j)),
        out_shape=jax.ShapeDtypeStruct((t, PROJ_WIDTH), F32),
        scratch_shapes=[pltpu.VMEM((tm, D_MODEL), BF16)],
        compiler_params=pltpu.CompilerParams(
            dimension_semantics=("parallel", "arbitrary"), vmem_limit_bytes=VMEM_LIMIT),
        name="inproj",
    )(x2, g, wp)


def _gla_kernel(qk_ref, v_ref, og_ref, gal_ref, w2_ref, ab_ref, gn_ref, y_ref, st_ref,
                qi_ref, ki_ref, qd_ref, kd_ref, dec_ref, *, n_chunks):
    @pl.when(pl.program_id(1) == 0)
    def _():
        st_ref[...] = jnp.zeros_like(st_ref)

    row = _iota((CHUNK, CHUNK), 0)
    col = _iota((CHUNK, CHUNK), 1)
    causal = row >= col
    tri = causal.astype(BF16)
    w2 = w2_ref[...]
    ab = ab_ref[...]
    gn = gn_ref[...]
    heads = range(GLA_HEADS)
    ks = [slice(h * GLA_DK, (h + 1) * GLA_DK) for h in heads]
    vs = [slice(h * GLA_DV, (h + 1) * GLA_DV) for h in heads]

    def rows_of(c):
        if isinstance(c, int):
            return pl.ds(c * CHUNK, CHUNK)
        return pl.ds(pl.multiple_of(c * CHUNK, CHUNK), CHUNK)

    def stage1_gate(c):
        return c, _mm(gal_ref[rows_of(c), :], w2)

    def stage1_cumsum(gate):
        c, lin = gate
        la = -_softplus(-(lin + ab)) * (1.0 / GLA_GATE_NORM)
        return c, _cumsum_rows(tri, la)

    def stage1_store(cs, slot):
        c, b = cs
        rows = rows_of(c)
        b_mid = b[CHUNK // 2 - 1:CHUNK // 2, :]
        b_last = b[CHUNK - 1:CHUNK, :]
        q = qk_ref[rows, 0:GLA_KEY] * (GLA_DK ** -0.5)
        k = qk_ref[rows, GLA_KEY:2 * GLA_KEY]
        qi_ref[slot] = (q * jnp.exp(b - b_mid)).astype(BF16)
        ki_ref[slot] = (k * jnp.exp(b_mid - b)).astype(BF16)
        qd_ref[slot] = (q * jnp.exp(b)).astype(BF16)
        kd_ref[slot] = (k * jnp.exp(b_last - b)).astype(BF16)
        dec_ref[slot] = jnp.exp(b_last)

    def stage2_front(c, slot):
        rows = rows_of(c)
        v = [v_ref[rows, vs[h]].astype(BF16) for h in heads]
        st = [st_ref[h] for h in heads]
        s = [_mm_nt(qi_ref[slot, :, ks[h]], ki_ref[slot, :, ks[h]]) for h in heads]
        inter = [_mm_nt(qd_ref[slot, :, ks[h]], st[h]) for h in heads]
        upd = [_mm_tn(v[h], kd_ref[slot, :, ks[h]]) for h in heads]
        return v, st, s, inter, upd

    def stage2_back(c, slot, front):
        v, st, s, inter, upd = front
        rows = rows_of(c)
        intra = [_mm(jnp.where(causal, s[h], 0.0), v[h]) for h in heads]
        for h in heads:
            st_ref[h] = st[h] * dec_ref[slot, :, ks[h]] + upd[h]
            o = intra[h] + inter[h]
            o = o * lax.rsqrt(jnp.mean(o * o, axis=-1, keepdims=True) + GLA_NORM_EPS) * gn
            og = og_ref[rows, vs[h]]
            y_ref[rows, vs[h]] = (o * (og * _sigmoid(og))).astype(BF16)

    stage1_store(stage1_cumsum(stage1_gate(0)), 0)

    def body(c, carry):
        slot = c & 1
        front = stage2_front(c, slot)
        gate = stage1_gate(c + 1)
        stage2_back(c, slot, front)
        stage1_store(stage1_cumsum(gate), 1 - slot)
        return carry

    lax.fori_loop(0, n_chunks - 1, body, 0)
    last_slot = (n_chunks - 1) & 1
    stage2_back(n_chunks - 1, last_slot, stage2_front(n_chunks - 1, last_slot))


def _gla(proj, w2p, ab, gn, *, batch, seq, lblk):
    t = batch * seq
    nsb = seq // lblk
    row_map = lambda cb: (lambda b, i: (b * nsb + i, cb))
    const = lambda b, i: (0, 0)
    return pl.pallas_call(
        functools.partial(_gla_kernel, n_chunks=lblk // CHUNK),
        grid=(batch, nsb),
        in_specs=[
            pl.BlockSpec((lblk, 2 * GLA_KEY), row_map(COL_GLA_QK // (2 * GLA_KEY))),
            pl.BlockSpec((lblk, GLA_VAL), row_map(COL_GLA_V // GLA_VAL)),
            pl.BlockSpec((lblk, GLA_VAL), row_map(COL_GLA_OG // GLA_VAL)),
            pl.BlockSpec((lblk, LANES), row_map(COL_GAL // LANES)),
            pl.BlockSpec((LANES, GLA_KEY), const),
            pl.BlockSpec((1, GLA_KEY), const),
            pl.BlockSpec((1, GLA_DV), const),
        ],
        out_specs=pl.BlockSpec((lblk, GLA_VAL), lambda b, i: (b * nsb + i, 0)),
        out_shape=jax.ShapeDtypeStruct((t, GLA_VAL), BF16),
        scratch_shapes=[
            pltpu.VMEM((GLA_HEADS, GLA_DV, GLA_DK), F32),
            pltpu.VMEM((2, CHUNK, GLA_KEY), BF16), pltpu.VMEM((2, CHUNK, GLA_KEY), BF16),
            pltpu.VMEM((2, CHUNK, GLA_KEY), BF16), pltpu.VMEM((2, CHUNK, GLA_KEY), BF16),
            pltpu.VMEM((2, 1, GLA_KEY), F32),
        ],
        compiler_params=pltpu.CompilerParams(
            dimension_semantics=("parallel", "arbitrary"), vmem_limit_bytes=VMEM_LIMIT),
        name="gla",
    )(proj, proj, proj, proj, w2p, ab, gn)


def _unit_lower_inverse(ns, masks, after_stage=None):
    after_stage = after_stage or {}
    stage = [0]

    def stage_done():
        hook = after_stage.get(stage[0])
        if hook is not None:
            hook()
        stage[0] += 1

    eye, blk16, off32, off64 = masks
    nd = [jnp.where(blk16, n, 0.0) for n in ns]
    x = [eye + d for d in nd]
    m = [_mm(d, d) for d in nd]
    stage_done()
    for step in range(3):
        xm = [_mm(xi, mi) for xi, mi in zip(x, m)]
        if step < 2:
            m = [_mm(mi, mi) for mi in m]
        x = [xi + xmi for xi, xmi in zip(x, xm)]
        stage_done()
    for off in (off32, off64):
        xn = [_mm(xi, jnp.where(off, n, 0.0)) for xi, n in zip(x, ns)]
        stage_done()
        xnx = [_mm(a, xi) for a, xi in zip(xn, x)]
        x = [xi + b for xi, b in zip(x, xnx)]
        stage_done()
    return x


def _rwkv_kernel(r_ref, k_ref, v_ref, lo_ref, mu_r_ref, mu_k_ref, mu_v_ref, mu_lo_ref,
                 w0_ref, a0_ref, kk_ref, ka_ref, rk_ref, lnw_ref, lnb_ref,
                 ww2_ref, aw2_ref, gw2_ref, y_ref,
                 pr_ref, pk_ref, pv_ref, plo_ref, st_ref,
                 n_ref, prm_ref, ahl_ref, bk2_ref, akv_ref, v2_ref, bonv_ref, gate_ref, elast_ref,
                 *, n_chunks, n_pairs):
    @pl.when(pl.program_id(2) == 0)
    def _():
        pr_ref[...] = jnp.zeros_like(pr_ref)
        pk_ref[...] = jnp.zeros_like(pk_ref)
        pv_ref[...] = jnp.zeros_like(pv_ref)
        plo_ref[...] = jnp.zeros_like(plo_ref)
        st_ref[...] = jnp.zeros_like(st_ref)

    width = n_pairs * LANES
    tri = (_iota((CHUNK, CHUNK), 0) >= _iota((CHUNK, CHUNK), 1)).astype(F32)

    srow = _iota((LANES, LANES), 0)
    scol = _iota((LANES, LANES), 1)
    same = (srow < RWKV_HEAD) == (scol < RWKV_HEAD)
    rtok = srow & (RWKV_HEAD - 1)
    ctok = scol & (RWKV_HEAD - 1)
    strict_same = (rtok > ctok) & same
    strict_other = (rtok > ctok) & (~same)
    incl = rtok >= ctok
    blk16 = (srow // 16) == (scol // 16)
    blk32 = (srow // 32) == (scol // 32)
    inv_masks = ((srow == scol).astype(F32), blk16, blk32 & (~blk16), same & (~blk32))

    lo = _iota((CHUNK, LANES), 1) < RWKV_HEAD
    first_w = _iota((CHUNK, width), 0) == 0
    first_lo = _iota((CHUNK, LORA_WIDTH), 0) == 0
    pairs = range(n_pairs)
    ls = [slice(p * LANES, (p + 1) * LANES) for p in pairs]

    def seg_sum(x):
        s0 = jnp.sum(jnp.where(lo, x, 0.0), axis=-1, keepdims=True)
        s1 = jnp.sum(jnp.where(lo, 0.0, x), axis=-1, keepdims=True)
        return jnp.where(lo, s0, s1)

    def halves(x):
        return jnp.where(lo, x[0:CHUNK], x[CHUNK:2 * CHUNK])

    def stack(a, b):
        return jnp.concatenate([a, b], axis=0)

    def shifted(z, prev_ref, first):
        zs = jnp.where(first, prev_ref[...], pltpu.roll(z, 1, axis=0))
        prev_ref[...] = z[CHUNK - 1:CHUNK, :]
        return zs

    def rows_of(c):
        if isinstance(c, int):
            return pl.ds(c * CHUNK, CHUNK)
        return pl.ds(pl.multiple_of(c * CHUNK, CHUNK), CHUNK)

    def stage1_lora(c):
        rows = rows_of(c)
        zl = lo_ref[rows, :]
        lora = zl + (shifted(zl, plo_ref, first_lo) - zl) * mu_lo_ref[...]
        wa = lora[:, 0:LANES]
        gl = lora[:, LANES:LORA_WIDTH]
        return (c, _mm(jnp.tanh(wa), ww2_ref[...]), _mm(wa, aw2_ref[...]),
                _mm(_sigmoid(gl), gw2_ref[...]))

    def stage1_decay(lora_out):
        c, w_lin, a_lin, gate_all = lora_out
        w_raw = -_softplus(-(w0_ref[...] + w_lin)) - 0.5
        logw_all = -jnp.exp(w_raw)
        return c, logw_all, a_lin, gate_all, _cumsum_rows(tri, logw_all)

    def stage1_front(c):
        return stage1_decay(stage1_lora(c))

    def stage1_back(front, slot):
        c, logw_all, a_lin, gate_all, lw_all = front
        rows = rows_of(c)
        zr = r_ref[rows, :]
        zk = k_ref[rows, :]
        zv = v_ref[rows, :]
        r_all = zr + (shifted(zr, pr_ref, first_w) - zr) * mu_r_ref[...]
        k_all = zk + (shifted(zk, pk_ref, first_w) - zk) * mu_k_ref[...]
        v_all = zv + (shifted(zv, pv_ref, first_w) - zv) * mu_v_ref[...]
        a_all = _sigmoid(a0_ref[...] + a_lin)

        r = [r_all[:, s] for s in ls]
        k = [k_all[:, s] for s in ls]
        v = [v_all[:, s] for s in ls]
        a = [a_all[:, s] for s in ls]
        lw = [lw_all[:, s] for s in ls]
        lwx = [lw[p] - logw_all[:, ls[p]] for p in pairs]
        kk = [k[p] * kk_ref[:, ls[p]] for p in pairs]
        kkn = [kk[p] / jnp.maximum(jnp.sqrt(seg_sum(kk[p] * kk[p])), 1e-12) for p in pairs]
        kmod = [k[p] * (1.0 + (a[p] - 1.0) * ka_ref[:, ls[p]]) for p in pairs]
        bvec = [kkn[p] * a[p] for p in pairs]
        mid = [x[CHUNK // 2 - 1:CHUNK // 2, :] for x in lw]
        last = [x[CHUNK - 1:CHUNK, :] for x in lw]
        ag = [-kkn[p] * jnp.exp(lwx[p] - mid[p]) for p in pairs]
        rg = [r[p] * jnp.exp(lw[p] - mid[p]) for p in pairs]
        e_ng = [jnp.exp(mid[p] - lw[p]) for p in pairs]
        bg = [bvec[p] * e_ng[p] for p in pairs]
        kg = [kmod[p] * e_ng[p] for p in pairs]
        pm0 = [_mm_nt(stack(jnp.where(lo, ag[p], 0.0), jnp.where(lo, rg[p], 0.0)), stack(bg[p], kg[p]))
               for p in pairs]
        pm1 = [_mm_nt(stack(jnp.where(lo, 0.0, ag[p]), jnp.where(lo, 0.0, rg[p])), stack(kg[p], bg[p]))
               for p in pairs]
        pa = [stack(pm0[p][0:CHUNK], pm1[p][0:CHUNK]) for p in pairs]
        akv = [_mm(jnp.where(strict_other, pa[p], 0.0), stack(v[p], v[p])) for p in pairs]
        for p in pairs:
            g_ls = jnp.exp(last[p] - lw[p])
            n_ref[slot, p] = jnp.where(strict_same, pa[p], 0.0)
            prm_ref[slot, p] = jnp.where(incl, stack(pm0[p][CHUNK:], pm1[p][CHUNK:]), 0.0).astype(BF16)
            akv_ref[slot, p] = halves(akv[p])
            ahl_ref[slot, p] = stack(-kkn[p] * jnp.exp(lwx[p]), r[p] * jnp.exp(lw[p])).astype(BF16)
            bk2_ref[slot, p] = stack(bvec[p] * g_ls, kmod[p] * g_ls).astype(BF16)
            v2_ref[slot, p] = v[p]
            bonv_ref[slot, p] = seg_sum(r[p] * kmod[p] * rk_ref[:, ls[p]]) * v[p]
            gate_ref[slot, p] = gate_all[:, ls[p]]
            elast_ref[slot, p] = jnp.exp(last[p])

    def stage2_front(slot, after_stage=None):
        st = [st_ref[p] for p in pairs]
        ah = [_mm_nt(ahl_ref[slot, p], st[p]) for p in pairs]
        tinv = _unit_lower_inverse([n_ref[slot, p] for p in pairs], inv_masks, after_stage)
        return st, ah, tinv

    def stage2_back(c, slot, front):
        st, ah, tinv = front
        rows = rows_of(c)
        w = [ah[p][0:CHUNK] + akv_ref[slot, p] for p in pairs]
        u = [halves(_mm(tinv[p], stack(w[p], w[p]))) for p in pairs]
        v = [v2_ref[slot, p] for p in pairs]
        ys = [_mm(prm_ref[slot, p], stack(jnp.where(lo, u[p], v[p]), jnp.where(lo, v[p], u[p])))
              for p in pairs]
        dh = [_mm_tn(stack(u[p], v[p]), bk2_ref[slot, p]) for p in pairs]
        for p in pairs:
            st_ref[p] = st[p] * elast_ref[slot, p] + jnp.where(same, dh[p], 0.0)
            y = ah[p][CHUNK:] + halves(ys[p])
            mean = seg_sum(y) * (1.0 / RWKV_HEAD)
            d = y - mean
            var = seg_sum(d * d) * (1.0 / RWKV_HEAD)
            yn = d * lax.rsqrt(var + RWKV_GN_EPS) * lnw_ref[:, ls[p]] + lnb_ref[:, ls[p]]
            y_ref[rows, ls[p]] = ((yn + bonv_ref[slot, p]) * gate_ref[slot, p]).astype(BF16)

    stage1_back(stage1_front(0), 0)

    def body(c, carry):
        slot = c & 1
        nxt = {}

        def emit_lora():
            nxt["lora"] = stage1_lora(c + 1)

        def emit_decay():
            nxt["front"] = stage1_decay(nxt["lora"])

        cur = stage2_front(slot, {0: emit_lora, 2: emit_decay})
        stage2_back(c, slot, cur)
        stage1_back(nxt["front"], 1 - slot)
        return carry

    lax.fori_loop(0, n_chunks - 1, body, 0)
    last_slot = (n_chunks - 1) & 1
    stage2_back(n_chunks - 1, last_slot, stage2_front(last_slot))


def _rwkv(proj, params, *, batch, seq, lblk, n_pairs):
    (mu_r, mu_k, mu_v, mu_lo, w0, a0, k_k, k_a, r_k, ln_w, ln_b, ww2p, aw2p, gw2) = params
    t = batch * seq
    nsb = seq // lblk
    width = n_pairs * LANES
    ngrp = RWKV_DIM // width
    row_map = lambda c0: (lambda b, g, i: (b * nsb + i, c0 // width + g))
    grp = lambda b, g, i: (0, g)
    const = lambda b, g, i: (0, 0)
    vec = pl.BlockSpec((1, width), grp)
    return pl.pallas_call(
        functools.partial(_rwkv_kernel, n_chunks=lblk // CHUNK, n_pairs=n_pairs),
        grid=(batch, ngrp, nsb),
        in_specs=[
            pl.BlockSpec((lblk, width), row_map(COL_R)),
            pl.BlockSpec((lblk, width), row_map(COL_K)),
            pl.BlockSpec((lblk, width), row_map(COL_V)),
            pl.BlockSpec((lblk, LORA_WIDTH), lambda b, g, i: (b * nsb + i, COL_LORA // LORA_WIDTH)),
            vec, vec, vec,
            pl.BlockSpec((1, LORA_WIDTH), const),
            vec, vec, vec, vec, vec, vec, vec,
            pl.BlockSpec((LANES, width), grp),
            pl.BlockSpec((LANES, width), grp),
            pl.BlockSpec((GATE_LORA, width), grp),
        ],
        out_specs=pl.BlockSpec((lblk, width), lambda b, g, i: (b * nsb + i, g)),
        out_shape=jax.ShapeDtypeStruct((t, RWKV_DIM), BF16),
        scratch_shapes=[
            pltpu.VMEM((1, width), F32), pltpu.VMEM((1, width), F32), pltpu.VMEM((1, width), F32),
            pltpu.VMEM((1, LORA_WIDTH), F32),
            pltpu.VMEM((n_pairs, LANES, LANES), F32),
            pltpu.VMEM((2, n_pairs, LANES, LANES), F32),
            pltpu.VMEM((2, n_pairs, LANES, LANES), BF16),
            pltpu.VMEM((2, n_pairs, LANES, LANES), BF16),
            pltpu.VMEM((2, n_pairs, LANES, LANES), BF16),
            pltpu.VMEM((2, n_pairs, CHUNK, LANES), F32),
            pltpu.VMEM((2, n_pairs, CHUNK, LANES), F32),
            pltpu.VMEM((2, n_pairs, CHUNK, LANES), F32),
            pltpu.VMEM((2, n_pairs, CHUNK, LANES), F32),
            pltpu.VMEM((2, n_pairs, 1, LANES), F32),
        ],
        compiler_params=pltpu.CompilerParams(
            dimension_semantics=("parallel", "parallel", "arbitrary"), vmem_limit_bytes=VMEM_LIMIT),
        name="rwkv",
    )(proj, proj, proj, proj, mu_r, mu_k, mu_v, mu_lo, w0, a0, k_k, k_a, r_k, ln_w, ln_b,
      ww2p, aw2p, gw2)


def _merge_kernel(x_ref, yg_ref, yr_ref, ga_ref, gb_ref, wg_ref, wr_ref, wo_ref, o_ref):
    mixed = (_sigmoid(ga_ref[...]) * jnp.dot(yg_ref[...], wg_ref[...], preferred_element_type=F32)
             + _sigmoid(gb_ref[...]) * jnp.dot(yr_ref[...], wr_ref[...], preferred_element_type=F32))
    o_ref[...] = x_ref[...] + _mm(mixed, wo_ref[...])


def _merge(x2, yg, yr, proj, wg, wr, wo, *, tm):
    t = x2.shape[0]
    tok = lambda i: (i, 0)
    const = lambda i: (0, 0)
    wspec = pl.BlockSpec((D_MODEL, D_MODEL), const)
    return pl.pallas_call(
        _merge_kernel,
        grid=(t // tm,),
        in_specs=[
            pl.BlockSpec((tm, D_MODEL), tok),
            pl.BlockSpec((tm, D_MODEL), tok),
            pl.BlockSpec((tm, D_MODEL), tok),
            pl.BlockSpec((tm, D_MODEL), lambda i: (i, COL_GATE // D_MODEL)),
            pl.BlockSpec((tm, D_MODEL), lambda i: (i, COL_GATE // D_MODEL + 1)),
            wspec, wspec, wspec,
        ],
        out_specs=pl.BlockSpec((tm, D_MODEL), tok),
        out_shape=jax.ShapeDtypeStruct((t, D_MODEL), F32),
        compiler_params=pltpu.CompilerParams(
            dimension_semantics=("parallel",), vmem_limit_bytes=VMEM_LIMIT),
        name="merge",
    )(x2, yg, yr, proj, proj, wg, wr, wo)


def _mlp_kernel(x_ref, g_ref, wu_ref, wd_ref, gf_ref, o_ref, *, final_norm):
    x = x_ref[...]
    h = (x * lax.rsqrt(jnp.mean(x * x, axis=-1, keepdims=True) + NORM_EPS) * g_ref[...]).astype(BF16)
    up = jnp.dot(h, wu_ref[...], preferred_element_type=F32)
    act = jnp.square(jnp.maximum(up, 0.0)).astype(BF16)
    x = x + jnp.dot(act, wd_ref[...], preferred_element_type=F32)
    if final_norm:
        x = x * lax.rsqrt(jnp.mean(x * x, axis=-1, keepdims=True) + NORM_EPS) * gf_ref[...]
    o_ref[...] = x


def _mlp(x2, g, wu, wd, gf, *, tm, final_norm):
    t = x2.shape[0]
    tok = lambda i: (i, 0)
    const = lambda i: (0, 0)
    return pl.pallas_call(
        functools.partial(_mlp_kernel, final_norm=final_norm),
        grid=(t // tm,),
        in_specs=[
            pl.BlockSpec((tm, D_MODEL), tok),
            pl.BlockSpec((1, D_MODEL), const),
            pl.BlockSpec((D_MODEL, D_FF), const),
            pl.BlockSpec((D_FF, D_MODEL), const),
            pl.BlockSpec((1, D_MODEL), const),
        ],
        out_specs=pl.BlockSpec((tm, D_MODEL), tok),
        out_shape=jax.ShapeDtypeStruct((t, D_MODEL), F32),
        compiler_params=pltpu.CompilerParams(
            dimension_semantics=("parallel",), vmem_limit_bytes=VMEM_LIMIT),
        name="mlp",
    )(x2, g, wu, wd, gf)


def _regroup_w_in(w):
    gla_w = w[:, :GLA_WIDTH]
    rw = w[:, GLA_WIDTH:GLA_WIDTH + RWKV_WIDTH]
    gates = w[:, GLA_WIDTH + RWKV_WIDTH:]
    gla_main = GLA_WIDTH - GLA_GATE_RANK
    pad = jnp.zeros((w.shape[0], PROJ_WIDTH - COL_GAL - GLA_GATE_RANK), w.dtype)
    return jnp.concatenate([gates, gla_w[:, :gla_main], rw, gla_w[:, gla_main:], pad], axis=1).astype(BF16)


def _row(v):
    return v.reshape(1, -1).astype(F32)


def kernel(x, norm_mix, w_in, gla_a_w2, gla_a_b, gla_norm, rwkv_mu, rwkv_w0, rwkv_w_w2, rwkv_a0, rwkv_a_w2, rwkv_g_w2, rwkv_k_k, rwkv_k_a, rwkv_r_k, rwkv_ln_w, rwkv_ln_b, w_branch_gla, w_branch_rwkv, w_out, norm_mlp, w_up, w_down, norm_final):
    batch, seq, d = x.shape
    assert d == D_MODEL and seq % CHUNK == 0
    t = batch * seq
    depth = norm_mix.shape[0]
    tm = 512 if t % 512 == 0 else CHUNK
    lblk = 1024 if seq % 1024 == 0 else CHUNK
    lblk_rwkv = lblk
    x2 = x.reshape(t, d)
    for l in range(depth):
        tn = PROJ_WIDTH // 4
        wp = _regroup_w_in(w_in[l]).reshape(D_MODEL, PROJ_WIDTH // tn, tn).transpose(1, 0, 2)
        proj = _inproj(x2, _row(norm_mix[l]), wp, tm=(1024 if t % 1024 == 0 else tm), tn=tn)

        w2p = jnp.zeros((LANES, GLA_KEY), BF16).at[:GLA_GATE_RANK].set(gla_a_w2[l].astype(BF16))
        y_gla = _gla(proj, w2p, _row(gla_a_b[l]), _row(gla_norm[l]), batch=batch, seq=seq, lblk=lblk)

        mu = rwkv_mu[l]
        zeros = jnp.zeros((DECAY_LORA, RWKV_DIM), BF16)
        ww2p = jnp.concatenate([rwkv_w_w2[l].astype(BF16), zeros], axis=0)
        aw2p = jnp.concatenate([zeros, rwkv_a_w2[l].astype(BF16)], axis=0)
        params = (_row(mu[0:RWKV_DIM]), _row(mu[RWKV_DIM:2 * RWKV_DIM]), _row(mu[2 * RWKV_DIM:3 * RWKV_DIM]),
                  _row(mu[3 * RWKV_DIM:]), _row(rwkv_w0[l]), _row(rwkv_a0[l]), _row(rwkv_k_k[l]),
                  _row(rwkv_k_a[l]), _row(rwkv_r_k[l]), _row(rwkv_ln_w[l]), _row(rwkv_ln_b[l]),
                  ww2p, aw2p, rwkv_g_w2[l].astype(BF16))
        y_rwkv = _rwkv(proj, params, batch=batch, seq=seq, lblk=lblk_rwkv, n_pairs=8)

        x2 = _merge(x2, y_gla, y_rwkv, proj, w_branch_gla[l].astype(BF16), w_branch_rwkv[l].astype(BF16),
                    w_out[l].astype(BF16), tm=tm)
        x2 = _mlp(x2, _row(norm_mlp[l]), w_up[l].astype(BF16), w_down[l].astype(BF16), _row(norm_final),
                  tm=tm, final_norm=(l == depth - 1))
    return x2.reshape(batch, seq, d)
```

```python
import functools

import jax
import jax.numpy as jnp
from jax import lax
from jax.experimental import pallas as pl
from jax.experimental.pallas import tpu as pltpu

F32 = jnp.float32
BF16 = jnp.bfloat16

D_MODEL = 1024
GLA_HEADS = 4
GLA_KEY = D_MODEL // 2
GLA_VAL = D_MODEL
GLA_DK = GLA_KEY // GLA_HEADS
GLA_DV = GLA_VAL // GLA_HEADS
GLA_GATE_RANK = 16
GLA_GATE_NORM = 16.0
GLA_NORM_EPS = 1e-5
RWKV_HEAD = 64
RWKV_DIM = D_MODEL
DECAY_LORA = 64
AAA_LORA = 64
GATE_LORA = 128
RWKV_GN_EPS = 64e-5
D_FF = 4 * D_MODEL
NORM_EPS = 1e-6

GLA_WIDTH = 2 * GLA_KEY + 2 * GLA_VAL + GLA_GATE_RANK
RWKV_WIDTH = 3 * RWKV_DIM + DECAY_LORA + AAA_LORA + GATE_LORA

LANES = 128
CHUNK = 64
COL_GATE = 0
COL_GLA_QK = 2048
COL_GLA_V = 3072
COL_GLA_OG = 4096
COL_R = 5120
COL_K = 6144
COL_V = 7168
COL_LORA = 8192
COL_GAL = 8448
PROJ_WIDTH = 8704
LORA_WIDTH = DECAY_LORA + AAA_LORA + GATE_LORA

VMEM_LIMIT = 56 * 1024 * 1024


def _mm(a, b):
    return jnp.dot(a.astype(BF16), b.astype(BF16), preferred_element_type=F32)


def _mm_nt(a, b):
    return lax.dot_general(a.astype(BF16), b.astype(BF16), (((1,), (1,)), ((), ())),
                           preferred_element_type=F32)


def _mm_tn(a, b):
    return lax.dot_general(a.astype(BF16), b.astype(BF16), (((0,), (0,)), ((), ())),
                           preferred_element_type=F32)


def _cumsum_rows(tri, x):
    tri = tri.astype(BF16)
    hi = x.astype(BF16)
    lo = (x - hi.astype(F32)).astype(BF16)
    return (jnp.dot(tri, hi, preferred_element_type=F32) + jnp.dot(tri, lo, preferred_element_type=F32))


def _softplus(x):
    return jnp.maximum(x, 0.0) + jnp.log(1.0 + jnp.exp(-jnp.abs(x)))


def _sigmoid(x):
    return 1.0 / (1.0 + jnp.exp(-x))


def _iota(shape, dim):
    return lax.broadcasted_iota(jnp.int32, shape, dim)


def _chunk_rows(c):
    if isinstance(c, int):
        return pl.ds(c * CHUNK, CHUNK)
    return pl.ds(pl.multiple_of(c * CHUNK, CHUNK), CHUNK)


def _inproj_kernel(x_ref, g_ref, w_ref, o_ref, h_ref):
    @pl.when(pl.program_id(1) == 0)
    def _():
        x = x_ref[...]
        ms = jnp.mean(x * x, axis=-1, keepdims=True)
        h_ref[...] = (x * lax.rsqrt(ms + NORM_EPS) * g_ref[...]).astype(BF16)

    o_ref[...] = jnp.dot(h_ref[...], w_ref[pl.program_id(1)], preferred_element_type=F32)


def _inproj(x2, g, wp, *, tm, tn):
    t = x2.shape[0]
    return pl.pallas_call(
        _inproj_kernel,
        grid=(t // tm, PROJ_WIDTH // tn),
        in_specs=[
            pl.BlockSpec((tm, D_MODEL), lambda i, j: (i, 0)),
            pl.BlockSpec((1, D_MODEL), lambda i, j: (0, 0)),
            pl.BlockSpec((PROJ_WIDTH // tn, D_MODEL, tn), lambda i, j: (0, 0, 0),
                         pipeline_mode=pl.Buffered(1)),
        ],
        out_specs=pl.BlockSpec((tm, tn), lambda i, j: (i, j)),
        out_shape=jax.ShapeDtypeStruct((t, PROJ_WIDTH), F32),
        scratch_shapes=[pltpu.VMEM((tm, D_MODEL), BF16)],
        compiler_params=pltpu.CompilerParams(
            dimension_semantics=("parallel", "arbitrary"), vmem_limit_bytes=VMEM_LIMIT),
        name="inproj",
    )(x2, g, wp)


def _gla_kernel(qk_ref, v_ref, og_ref, gal_ref, w2_ref, ab_ref, gn_ref, y_ref, st_ref,
                qi_ref, ki_ref, qd_ref, kd_ref, dec_ref, *, n_chunks):
    @pl.when(pl.program_id(1) == 0)
    def _():
        st_ref[...] = jnp.zeros_like(st_ref)

    row = _iota((CHUNK, CHUNK), 0)
    col = _iota((CHUNK, CHUNK), 1)
    causal = row >= col
    tri = causal.astype(BF16)
    w2 = w2_ref[...]
    ab = ab_ref[...]
    gn = gn_ref[...]
    heads = range(GLA_HEADS)
    ks = [slice(h * GLA_DK, (h + 1) * GLA_DK) for h in heads]
    vs = [slice(h * GLA_DV, (h + 1) * GLA_DV) for h in heads]

    def stage1_gate(c):
        return c, _mm(gal_ref[_chunk_rows(c), :], w2)

    def stage1_cumsum(gate):
        c, lin = gate
        la = -_softplus(-(lin + ab)) * (1.0 / GLA_GATE_NORM)
        return c, _cumsum_rows(tri, la)

    def stage1_store(cs, slot):
        c, b = cs
        rows = _chunk_rows(c)
        b_mid = b[CHUNK // 2 - 1:CHUNK // 2, :]
        b_last = b[CHUNK - 1:CHUNK, :]
        q = qk_ref[rows, 0:GLA_KEY] * (GLA_DK ** -0.5)
        k = qk_ref[rows, GLA_KEY:2 * GLA_KEY]
        qi_ref[slot] = (q * jnp.exp(b - b_mid)).astype(BF16)
        ki_ref[slot] = (k * jnp.exp(b_mid - b)).astype(BF16)
        qd_ref[slot] = (q * jnp.exp(b)).astype(BF16)
        kd_ref[slot] = (k * jnp.exp(b_last - b)).astype(BF16)
        dec_ref[slot] = jnp.exp(b_last)

    def stage2_front(c, slot):
        rows = _chunk_rows(c)
        v = [v_ref[rows, vs[h]].astype(BF16) for h in heads]
        st = [st_ref[h] for h in heads]
        s = [_mm_nt(qi_ref[slot, :, ks[h]], ki_ref[slot, :, ks[h]]) for h in heads]
        inter = [_mm_nt(qd_ref[slot, :, ks[h]], st[h]) for h in heads]
        upd = [_mm_tn(v[h], kd_ref[slot, :, ks[h]]) for h in heads]
        return v, st, s, inter, upd

    def stage2_back(c, slot, front):
        v, st, s, inter, upd = front
        rows = _chunk_rows(c)
        intra = [_mm(jnp.where(causal, s[h], 0.0), v[h]) for h in heads]
        for h in heads:
            st_ref[h] = st[h] * dec_ref[slot, :, ks[h]] + upd[h]
            o = intra[h] + inter[h]
            o = o * lax.rsqrt(jnp.mean(o * o, axis=-1, keepdims=True) + GLA_NORM_EPS) * gn
            og = og_ref[rows, vs[h]]
            y_ref[rows, vs[h]] = (o * (og * _sigmoid(og))).astype(BF16)

    stage1_store(stage1_cumsum(stage1_gate(0)), 0)

    def body(c, carry):
        slot = c & 1
        front = stage2_front(c, slot)
        gate = stage1_gate(c + 1)
        stage2_back(c, slot, front)
        stage1_store(stage1_cumsum(gate), 1 - slot)
        return carry

    lax.fori_loop(0, n_chunks - 1, body, 0)
    last_slot = (n_chunks - 1) & 1
    stage2_back(n_chunks - 1, last_slot, stage2_front(n_chunks - 1, last_slot))


def _gla(proj, w2p, ab, gn, *, batch, seq, lblk):
    t = batch * seq
    nsb = seq // lblk
    row_map = lambda cb: (lambda b, i: (b * nsb + i, cb))
    const = lambda b, i: (0, 0)
    return pl.pallas_call(
        functools.partial(_gla_kernel, n_chunks=lblk // CHUNK),
        grid=(batch, nsb),
        in_specs=[
            pl.BlockSpec((lblk, 2 * GLA_KEY), row_map(COL_GLA_QK // (2 * GLA_KEY))),
            pl.BlockSpec((lblk, GLA_VAL), row_map(COL_GLA_V // GLA_VAL)),
            pl.BlockSpec((lblk, GLA_VAL), row_map(COL_GLA_OG // GLA_VAL)),
            pl.BlockSpec((lblk, LANES), row_map(COL_GAL // LANES)),
            pl.BlockSpec((LANES, GLA_KEY), const),
            pl.BlockSpec((1, GLA_KEY), const),
            pl.BlockSpec((1, GLA_DV), const),
        ],
        out_specs=pl.BlockSpec((lblk, GLA_VAL), lambda b, i: (b * nsb + i, 0)),
        out_shape=jax.ShapeDtypeStruct((t, GLA_VAL), BF16),
        scratch_shapes=[
            pltpu.VMEM((GLA_HEADS, GLA_DV, GLA_DK), F32),
            pltpu.VMEM((2, CHUNK, GLA_KEY), BF16), pltpu.VMEM((2, CHUNK, GLA_KEY), BF16),
            pltpu.VMEM((2, CHUNK, GLA_KEY), BF16), pltpu.VMEM((2, CHUNK, GLA_KEY), BF16),
            pltpu.VMEM((2, 1, GLA_KEY), F32),
        ],
        compiler_params=pltpu.CompilerParams(
            dimension_semantics=("parallel", "arbitrary"), vmem_limit_bytes=VMEM_LIMIT),
        name="gla",
    )(proj, proj, proj, proj, w2p, ab, gn)


def _pair_block_diag(y, lo):
    zero = jnp.zeros_like(y)
    return jnp.concatenate([jnp.where(lo, y, zero), jnp.where(lo, zero, y)], axis=0)


def _unit_lower_inverse(ns, masks, after_stage=None):
    after_stage = after_stage or {}
    stage = [0]

    def stage_done():
        hook = after_stage.get(stage[0])
        if hook is not None:
            hook()
        stage[0] += 1

    eye, blk16, off32, off64, lo = masks
    nd = [jnp.where(blk16, n, 0.0) for n in ns]
    x = [eye + d for d in nd]
    nd = [d.astype(BF16) for d in nd]
    m = [_mm(d, _pair_block_diag(d, lo)).astype(BF16) for d in nd]
    stage_done()
    for step in range(3):
        mb = [_pair_block_diag(mi, lo) for mi in m]
        xm = [_mm(xi, mbi) for xi, mbi in zip(x, mb)]
        if step < 2:
            m = [_mm(mi, mbi).astype(BF16) for mi, mbi in zip(m, mb)]
        x = [xi + xmi for xi, xmi in zip(x, xm)]
        stage_done()
    for off in (off32, off64):
        xb = [xi.astype(BF16) for xi in x]
        xn = [_mm(xi, _pair_block_diag(jnp.where(off, n, 0.0).astype(BF16), lo)) for xi, n in zip(xb, ns)]
        stage_done()
        xnx = [_mm(a, _pair_block_diag(xi, lo)) for a, xi in zip(xn, xb)]
        x = [xi + b for xi, b in zip(x, xnx)]
        stage_done()
    return x


def _rwkv_kernel(r_ref, k_ref, v_ref, lo_ref, mu_r_ref, mu_k_ref, mu_v_ref, mu_lo_ref,
                 w0_ref, a0_ref, kk_ref, ka_ref, rk_ref, lnw_ref, lnb_ref,
                 ww2_ref, aw2_ref, gw2_ref, y_ref,
                 pr_ref, pk_ref, pv_ref, plo_ref, st_ref,
                 n_ref, prm_ref, ahl_ref, bk2_ref, vx_ref, akv_ref, v2_ref, bonv_ref, gate_ref, elast_ref,
                 t_ref,
                 *, n_chunks, n_pairs):
    @pl.when(pl.program_id(2) == 0)
    def _():
        pr_ref[...] = jnp.zeros_like(pr_ref)
        pk_ref[...] = jnp.zeros_like(pk_ref)
        pv_ref[...] = jnp.zeros_like(pv_ref)
        plo_ref[...] = jnp.zeros_like(plo_ref)
        st_ref[...] = jnp.zeros_like(st_ref)

    width = n_pairs * LANES
    tri = (_iota((CHUNK, CHUNK), 0) >= _iota((CHUNK, CHUNK), 1)).astype(BF16)

    trow = _iota((CHUNK, LANES), 0)
    tcol = _iota((CHUNK, LANES), 1) & (RWKV_HEAD - 1)
    lo = _iota((CHUNK, LANES), 1) < RWKV_HEAD
    strict = trow > tcol
    incl = trow >= tcol
    blk16 = (trow // 16) == (tcol // 16)
    blk32 = (trow // 32) == (tcol // 32)
    inv_masks = ((trow == tcol).astype(F32), blk16, blk32 & (~blk16), ~blk32, lo)
    srow = _iota((LANES, LANES), 0)
    scol = _iota((LANES, LANES), 1)
    same = (srow < RWKV_HEAD) == (scol < RWKV_HEAD)
    first_w = _iota((CHUNK, width), 0) == 0
    first_lo = _iota((CHUNK, LORA_WIDTH), 0) == 0
    pairs = range(n_pairs)
    ls = [slice(p * LANES, (p + 1) * LANES) for p in pairs]

    def seg_sum(x):
        s0 = jnp.sum(jnp.where(lo, x, 0.0), axis=-1, keepdims=True)
        s1 = jnp.sum(jnp.where(lo, 0.0, x), axis=-1, keepdims=True)
        return jnp.where(lo, s0, s1)

    def stack(a, b):
        return jnp.concatenate([a, b], axis=0)

    def shifted(z, prev_ref, first):
        zs = jnp.where(first, prev_ref[...], pltpu.roll(z, 1, axis=0))
        prev_ref[...] = z[CHUNK - 1:CHUNK, :]
        return zs

    def stage1_lora(c):
        rows = _chunk_rows(c)
        zl = lo_ref[rows, :]
        lora = zl + (shifted(zl, plo_ref, first_lo) - zl) * mu_lo_ref[...]
        wa = lora[:, 0:LANES]
        gl = lora[:, LANES:LORA_WIDTH]
        return (c, _mm(jnp.tanh(wa), ww2_ref[...]), _mm(wa, aw2_ref[...]),
                _mm(_sigmoid(gl), gw2_ref[...]))

    def stage1_decay(lora_out):
        c, w_lin, a_lin, gate_all = lora_out
        w_raw = -_softplus(-(w0_ref[...] + w_lin)) - 0.5
        logw_all = -jnp.exp(w_raw)
        return c, logw_all, a_lin, gate_all, _cumsum_rows(tri, logw_all)

    def stage1_back(front, slot):
        c, logw_all, a_lin, gate_all, lw_all = front
        rows = _chunk_rows(c)
        zr = r_ref[rows, :]
        zk = k_ref[rows, :]
        zv = v_ref[rows, :]
        r_all = zr + (shifted(zr, pr_ref, first_w) - zr) * mu_r_ref[...]
        k_all = zk + (shifted(zk, pk_ref, first_w) - zk) * mu_k_ref[...]
        v_all = zv + (shifted(zv, pv_ref, first_w) - zv) * mu_v_ref[...]
        a_all = _sigmoid(a0_ref[...] + a_lin)

        r = [r_all[:, s] for s in ls]
        k = [k_all[:, s] for s in ls]
        v = [v_all[:, s] for s in ls]
        a = [a_all[:, s] for s in ls]
        lw = [lw_all[:, s] for s in ls]
        lwx = [lw[p] - logw_all[:, ls[p]] for p in pairs]
        kk = [k[p] * kk_ref[:, ls[p]] for p in pairs]
        kkn = [kk[p] / jnp.maximum(jnp.sqrt(seg_sum(kk[p] * kk[p])), 1e-12) for p in pairs]
        kmod = [k[p] * (1.0 + (a[p] - 1.0) * ka_ref[:, ls[p]]) for p in pairs]
        bvec = [kkn[p] * a[p] for p in pairs]
        mid = [x[CHUNK // 2 - 1:CHUNK // 2, :] for x in lw]
        last = [x[CHUNK - 1:CHUNK, :] for x in lw]
        zero = jnp.zeros((CHUNK, LANES), BF16)
        ag = [(-kkn[p] * jnp.exp(lwx[p] - mid[p])).astype(BF16) for p in pairs]
        rg = [(r[p] * jnp.exp(lw[p] - mid[p])).astype(BF16) for p in pairs]
        e_ng = [jnp.exp(mid[p] - lw[p]) for p in pairs]
        bg = [(bvec[p] * e_ng[p]).astype(BF16) for p in pairs]
        kg = [(kmod[p] * e_ng[p]).astype(BF16) for p in pairs]
        v16 = [v[p].astype(BF16) for p in pairs]
        pm0 = [_mm_nt(stack(jnp.where(lo, ag[p], zero), jnp.where(lo, rg[p], zero)), stack(bg[p], kg[p]))
               for p in pairs]
        pm1 = [_mm_nt(stack(jnp.where(lo, zero, ag[p]), jnp.where(lo, zero, rg[p])), stack(kg[p], bg[p]))
               for p in pairs]
        ak = [jnp.where(strict, jnp.where(lo, pm1[p][0:CHUNK], pm0[p][0:CHUNK]), 0.0) for p in pairs]
        vx = [stack(jnp.where(lo, zero, v16[p]), jnp.where(lo, v16[p], zero)) for p in pairs]
        akv = [_mm(ak[p], vx[p]) for p in pairs]
        for p in pairs:
            g_ls = jnp.exp(last[p] - lw[p])
            n_ref[slot, p] = jnp.where(strict, jnp.where(lo, pm0[p][0:CHUNK], pm1[p][0:CHUNK]), 0.0)
            rb = jnp.where(incl, jnp.where(lo, pm0[p][CHUNK:], pm1[p][CHUNK:]), 0.0)
            rk = jnp.where(incl, jnp.where(lo, pm1[p][CHUNK:], pm0[p][CHUNK:]), 0.0)
            prm_ref[slot, p] = jnp.concatenate([rb, rk], axis=1).astype(BF16)
            vx_ref[slot, p] = vx[p]
            akv_ref[slot, p] = akv[p]
            ahl_ref[slot, p] = stack(-kkn[p] * jnp.exp(lwx[p]), r[p] * jnp.exp(lw[p])).astype(BF16)
            bk2_ref[slot, p] = stack(bvec[p] * g_ls, kmod[p] * g_ls).astype(BF16)
            v2_ref[slot, p] = v16[p]
            bonv_ref[slot, p] = seg_sum(r[p] * kmod[p] * rk_ref[:, ls[p]]) * v[p]
            gate_ref[slot, p] = gate_all[:, ls[p]]
            elast_ref[slot, p] = jnp.exp(last[p])

    def stage2(slot, tslot, after_stage=None):
        tinv = _unit_lower_inverse([n_ref[slot, p] for p in pairs], inv_masks, after_stage)
        for p in pairs:
            t_ref[tslot, p] = tinv[p].astype(BF16)

    def stage3_state(slot):
        st = [st_ref[p] for p in pairs]
        ah = [_mm_nt(ahl_ref[slot, p], st[p]) for p in pairs]
        return st, ah

    def stage3_solve(slot, tslot, front):
        st, ah = front
        w = [(ah[p][0:CHUNK] + akv_ref[slot, p]).astype(BF16) for p in pairs]
        return [_mm(t_ref[tslot, p], _pair_block_diag(w[p], lo)).astype(BF16) for p in pairs]

    def stage3_out(c, slot, front, u):
        st, ah = front
        rows = _chunk_rows(c)
        ys = [_mm(prm_ref[slot, p], stack(_pair_block_diag(u[p], lo), vx_ref[slot, p])) for p in pairs]
        dh = [_mm_tn(stack(u[p], v2_ref[slot, p]), bk2_ref[slot, p]) for p in pairs]
        for p in pairs:
            st_ref[p] = st[p] * elast_ref[slot, p] + jnp.where(same, dh[p], 0.0)
            y = ah[p][CHUNK:] + ys[p]
            mean = seg_sum(y) * (1.0 / RWKV_HEAD)
            d = y - mean
            var = seg_sum(d * d) * (1.0 / RWKV_HEAD)
            yn = d * lax.rsqrt(var + RWKV_GN_EPS) * lnw_ref[:, ls[p]] + lnb_ref[:, ls[p]]
            y_ref[rows, ls[p]] = ((yn + bonv_ref[slot, p]) * gate_ref[slot, p]).astype(BF16)

    def step(c):
        s1, s2, s3 = [0 <= c + k < n_chunks if isinstance(c, int) else True for k in (2, 1, 0)]
        slot3, slot2, slot1 = c % 3, (c + 1) % 3, (c + 2) % 3
        tslot3, tslot2 = c & 1, (c + 1) & 1
        held = {}

        def after0():
            if s1:
                held["lora"] = stage1_lora(c + 2)
            if s3:
                held["u"] = stage3_solve(slot3, tslot3, held["state"])

        def after1():
            if s3:
                stage3_out(c, slot3, held["state"], held["u"])

        def after2():
            if s1:
                held["front"] = stage1_decay(held["lora"])

        if s3:
            held["state"] = stage3_state(slot3)
        if s2:
            stage2(slot2, tslot2, {0: after0, 1: after1, 2: after2})
        else:
            after0(), after1(), after2()
        if s1:
            stage1_back(held["front"], slot1)

    n_steady = max(n_chunks - 2, 0)
    for c in range(-2, 0):
        step(c)
    if n_steady > 0:
        def body(c, carry):
            step(c)
            return carry

        lax.fori_loop(0, n_steady, body, 0)
    for c in range(n_steady, n_chunks):
        step(c)


def _rwkv(proj, params, *, batch, seq, lblk, n_pairs):
    (mu_r, mu_k, mu_v, mu_lo, w0, a0, k_k, k_a, r_k, ln_w, ln_b, ww2p, aw2p, gw2) = params
    t = batch * seq
    nsb = seq // lblk
    width = n_pairs * LANES
    ngrp = RWKV_DIM // width
    row_map = lambda c0: (lambda b, g, i: (b * nsb + i, c0 // width + g))
    grp = lambda b, g, i: (0, g)
    const = lambda b, g, i: (0, 0)
    vec = pl.BlockSpec((1, width), grp)
    rec = lambda slots, rows, cols, dtype: pltpu.VMEM((slots, n_pairs, rows, cols), dtype)
    return pl.pallas_call(
        functools.partial(_rwkv_kernel, n_chunks=lblk // CHUNK, n_pairs=n_pairs),
        grid=(batch, ngrp, nsb),
        in_specs=[
            pl.BlockSpec((lblk, width), row_map(COL_R)),
            pl.BlockSpec((lblk, width), row_map(COL_K)),
            pl.BlockSpec((lblk, width), row_map(COL_V)),
            pl.BlockSpec((lblk, LORA_WIDTH), lambda b, g, i: (b * nsb + i, COL_LORA // LORA_WIDTH)),
            vec, vec, vec,
            pl.BlockSpec((1, LORA_WIDTH), const),
            vec, vec, vec, vec, vec, vec, vec,
            pl.BlockSpec((LANES, width), grp),
            pl.BlockSpec((LANES, width), grp),
            pl.BlockSpec((GATE_LORA, width), grp),
        ],
        out_specs=pl.BlockSpec((lblk, width), lambda b, g, i: (b * nsb + i, g)),
        out_shape=jax.ShapeDtypeStruct((t, RWKV_DIM), BF16),
        scratch_shapes=[
            pltpu.VMEM((1, width), F32), pltpu.VMEM((1, width), F32), pltpu.VMEM((1, width), F32),
            pltpu.VMEM((1, LORA_WIDTH), F32),
            pltpu.VMEM((n_pairs, LANES, LANES), F32),
            rec(3, CHUNK, LANES, F32),
            rec(3, CHUNK, 2 * LANES, BF16),
            rec(3, LANES, LANES, BF16),
            rec(3, LANES, LANES, BF16),
            rec(3, LANES, LANES, BF16),
            rec(3, CHUNK, LANES, F32),
            rec(3, CHUNK, LANES, BF16),
            rec(3, CHUNK, LANES, F32),
            rec(3, CHUNK, LANES, F32),
            rec(3, 1, LANES, F32),
            rec(2, CHUNK, LANES, BF16),
        ],
        compiler_params=pltpu.CompilerParams(
            dimension_semantics=("parallel", "parallel", "arbitrary"), vmem_limit_bytes=VMEM_LIMIT),
        name="rwkv",
    )(proj, proj, proj, proj, mu_r, mu_k, mu_v, mu_lo, w0, a0, k_k, k_a, r_k, ln_w, ln_b,
      ww2p, aw2p, gw2)


def _merge_kernel(x_ref, yg_ref, yr_ref, ga_ref, gb_ref, wg_ref, wr_ref, wo_ref, o_ref):
    mixed = (_sigmoid(ga_ref[...]) * jnp.dot(yg_ref[...], wg_ref[...], preferred_element_type=F32)
             + _sigmoid(gb_ref[...]) * jnp.dot(yr_ref[...], wr_ref[...], preferred_element_type=F32))
    o_ref[...] = x_ref[...] + _mm(mixed, wo_ref[...])


def _merge(x2, yg, yr, proj, wg, wr, wo, *, tm):
    t = x2.shape[0]
    tok = lambda i: (i, 0)
    const = lambda i: (0, 0)
    wspec = pl.BlockSpec((D_MODEL, D_MODEL), const)
    return pl.pallas_call(
        _merge_kernel,
        grid=(t // tm,),
        in_specs=[
            pl.BlockSpec((tm, D_MODEL), tok),
            pl.BlockSpec((tm, D_MODEL), tok),
            pl.BlockSpec((tm, D_MODEL), tok),
            pl.BlockSpec((tm, D_MODEL), lambda i: (i, COL_GATE // D_MODEL)),
            pl.BlockSpec((tm, D_MODEL), lambda i: (i, COL_GATE // D_MODEL + 1)),
            wspec, wspec, wspec,
        ],
        out_specs=pl.BlockSpec((tm, D_MODEL), tok),
        out_shape=jax.ShapeDtypeStruct((t, D_MODEL), F32),
        compiler_params=pltpu.CompilerParams(
            dimension_semantics=("parallel",), vmem_limit_bytes=VMEM_LIMIT),
        name="merge",
    )(x2, yg, yr, proj, proj, wg, wr, wo)


def _mlp_kernel(x_ref, g_ref, wu_ref, wd_ref, gf_ref, o_ref, *, final_norm):
    x = x_ref[...]
    h = (x * lax.rsqrt(jnp.mean(x * x, axis=-1, keepdims=True) + NORM_EPS) * g_ref[...]).astype(BF16)
    up = jnp.dot(h, wu_ref[...], preferred_element_type=F32)
    act = jnp.square(jnp.maximum(up, 0.0)).astype(BF16)
    x = x + jnp.dot(act, wd_ref[...], preferred_element_type=F32)
    if final_norm:
        x = x * lax.rsqrt(jnp.mean(x * x, axis=-1, keepdims=True) + NORM_EPS) * gf_ref[...]
    o_ref[...] = x


def _mlp(x2, g, wu, wd, gf, *, tm, final_norm):
    t = x2.shape[0]
    tok = lambda i: (i, 0)
    const = lambda i: (0, 0)
    return pl.pallas_call(
        functools.partial(_mlp_kernel, final_norm=final_norm),
        grid=(t // tm,),
        in_specs=[
            pl.BlockSpec((tm, D_MODEL), tok),
            pl.BlockSpec((1, D_MODEL), const),
            pl.BlockSpec((D_MODEL, D_FF), const),
            pl.BlockSpec((D_FF, D_MODEL), const),
            pl.BlockSpec((1, D_MODEL), const),
        ],
        out_specs=pl.BlockSpec((tm, D_MODEL), tok),
        out_shape=jax.ShapeDtypeStruct((t, D_MODEL), F32),
        compiler_params=pltpu.CompilerParams(
            dimension_semantics=("parallel",), vmem_limit_bytes=VMEM_LIMIT),
        name="mlp",
    )(x2, g, wu, wd, gf)


def _regroup_w_in(w):
    gla_w = w[:, :GLA_WIDTH]
    rw = w[:, GLA_WIDTH:GLA_WIDTH + RWKV_WIDTH]
    gates = w[:, GLA_WIDTH + RWKV_WIDTH:]
    gla_main = GLA_WIDTH - GLA_GATE_RANK
    pad = jnp.zeros((w.shape[0], PROJ_WIDTH - COL_GAL - GLA_GATE_RANK), w.dtype)
    return jnp.concatenate([gates, gla_w[:, :gla_main], rw, gla_w[:, gla_main:], pad], axis=1).astype(BF16)


def _row(v):
    return v.reshape(1, -1).astype(F32)


def kernel(x, norm_mix, w_in, gla_a_w2, gla_a_b, gla_norm, rwkv_mu, rwkv_w0, rwkv_w_w2, rwkv_a0, rwkv_a_w2, rwkv_g_w2, rwkv_k_k, rwkv_k_a, rwkv_r_k, rwkv_ln_w, rwkv_ln_b, w_branch_gla, w_branch_rwkv, w_out, norm_mlp, w_up, w_down, norm_final):
    batch, seq, d = x.shape
    assert d == D_MODEL and seq % CHUNK == 0
    t = batch * seq
    depth = norm_mix.shape[0]
    tm = 512 if t % 512 == 0 else CHUNK
    lblk = 1024 if seq % 1024 == 0 else CHUNK
    x2 = x.reshape(t, d)
    for l in range(depth):
        tn = PROJ_WIDTH // 4
        wp = _regroup_w_in(w_in[l]).reshape(D_MODEL, PROJ_WIDTH // tn, tn).transpose(1, 0, 2)
        proj = _inproj(x2, _row(norm_mix[l]), wp, tm=(1024 if t % 1024 == 0 else tm), tn=tn)

        w2p = jnp.zeros((LANES, GLA_KEY), BF16).at[:GLA_GATE_RANK].set(gla_a_w2[l].astype(BF16))
        y_gla = _gla(proj, w2p, _row(gla_a_b[l]), _row(gla_norm[l]), batch=batch, seq=seq, lblk=lblk)

        mu = rwkv_mu[l]
        zeros = jnp.zeros((DECAY_LORA, RWKV_DIM), BF16)
        ww2p = jnp.concatenate([rwkv_w_w2[l].astype(BF16), zeros], axis=0)
        aw2p = jnp.concatenate([zeros, rwkv_a_w2[l].astype(BF16)], axis=0)
        params = (_row(mu[0:RWKV_DIM]), _row(mu[RWKV_DIM:2 * RWKV_DIM]), _row(mu[2 * RWKV_DIM:3 * RWKV_DIM]),
                  _row(mu[3 * RWKV_DIM:]), _row(rwkv_w0[l]), _row(rwkv_a0[l]), _row(rwkv_k_k[l]),
                  _row(rwkv_k_a[l]), _row(rwkv_r_k[l]), _row(rwkv_ln_w[l]), _row(rwkv_ln_b[l]),
                  ww2p, aw2p, rwkv_g_w2[l].astype(BF16))
        y_rwkv = _rwkv(proj, params, batch=batch, seq=seq, lblk=lblk, n_pairs=RWKV_DIM // LANES)

        x2 = _merge(x2, y_gla, y_rwkv, proj, w_branch_gla[l].astype(BF16), w_branch_rwkv[l].astype(BF16),
                    w_out[l].astype(BF16), tm=tm)
        x2 = _mlp(x2, _row(norm_mlp[l]), w_up[l].astype(BF16), w_down[l].astype(BF16), _row(norm_final),
                  tm=tm, final_norm=(l == depth - 1))
    return x2.reshape(batch, seq, d)
```

```python
import functools

import jax
import jax.numpy as jnp
from jax import lax
from jax.experimental import pallas as pl
from jax.experimental.pallas import tpu as pltpu

F32 = jnp.float32
BF16 = jnp.bfloat16

D_MODEL = 1024
GLA_HEADS = 4
GLA_KEY = D_MODEL // 2
GLA_VAL = D_MODEL
GLA_DK = GLA_KEY // GLA_HEADS
GLA_DV = GLA_VAL // GLA_HEADS
GLA_GATE_RANK = 16
GLA_GATE_NORM = 16.0
GLA_NORM_EPS = 1e-5
RWKV_HEAD = 64
RWKV_DIM = D_MODEL
DECAY_LORA = 64
AAA_LORA = 64
GATE_LORA = 128
RWKV_GN_EPS = 64e-5
KK_NORM_FLOOR = 1e-12
DECAY_SCALE = 0.6065306597126334
D_FF = 4 * D_MODEL
NORM_EPS = 1e-6

GLA_WIDTH = 2 * GLA_KEY + 2 * GLA_VAL + GLA_GATE_RANK
RWKV_WIDTH = 3 * RWKV_DIM + DECAY_LORA + AAA_LORA + GATE_LORA

LANES = 128
CHUNK = 64
COL_GATE = 0
COL_GLA_QK = 2048
COL_GLA_V = 3072
COL_GLA_OG = 4096
COL_R = 5120
COL_K = 6144
COL_V = 7168
COL_LORA = 8192
COL_GAL = 8448
PROJ_WIDTH = 8704
LORA_WIDTH = DECAY_LORA + AAA_LORA + GATE_LORA

VMEM_LIMIT = 56 * 1024 * 1024


def _mm(a, b):
    return jnp.dot(a.astype(BF16), b.astype(BF16), preferred_element_type=F32)


def _mm_nt(a, b):
    return lax.dot_general(a.astype(BF16), b.astype(BF16), (((1,), (1,)), ((), ())),
                           preferred_element_type=F32)


def _mm_tn(a, b):
    return lax.dot_general(a.astype(BF16), b.astype(BF16), (((0,), (0,)), ((), ())),
                           preferred_element_type=F32)


def _cumsum_rows(tri, x):
    tri = tri.astype(BF16)
    hi = x.astype(BF16)
    lo = (x - hi.astype(F32)).astype(BF16)
    return (jnp.dot(tri, hi, preferred_element_type=F32) + jnp.dot(tri, lo, preferred_element_type=F32))


def _softplus(x):
    return jnp.maximum(x, 0.0) + jnp.log(1.0 + jnp.exp(-jnp.abs(x)))


def _sigmoid(x):
    return 1.0 / (1.0 + jnp.exp(-x))


def _iota(shape, dim):
    return lax.broadcasted_iota(jnp.int32, shape, dim)


def _chunk_rows(c):
    if isinstance(c, int):
        return pl.ds(c * CHUNK, CHUNK)
    return pl.ds(pl.multiple_of(c * CHUNK, CHUNK), CHUNK)


def _inproj_kernel(x_ref, g_ref, w_ref, o_ref, h_ref):
    @pl.when(pl.program_id(1) == 0)
    def _():
        x = x_ref[...]
        ms = jnp.mean(x * x, axis=-1, keepdims=True)
        h_ref[...] = (x * lax.rsqrt(ms + NORM_EPS) * g_ref[...]).astype(BF16)

    o_ref[...] = jnp.dot(h_ref[...], w_ref[pl.program_id(1)], preferred_element_type=F32)


def _inproj(x2, g, wp, *, tm, tn):
    t = x2.shape[0]
    return pl.pallas_call(
        _inproj_kernel,
        grid=(t // tm, PROJ_WIDTH // tn),
        in_specs=[
            pl.BlockSpec((tm, D_MODEL), lambda i, j: (i, 0)),
            pl.BlockSpec((1, D_MODEL), lambda i, j: (0, 0)),
            pl.BlockSpec((PROJ_WIDTH // tn, D_MODEL, tn), lambda i, j: (0, 0, 0),
                         pipeline_mode=pl.Buffered(1)),
        ],
        out_specs=pl.BlockSpec((tm, tn), lambda i, j: (i, j)),
        out_shape=jax.ShapeDtypeStruct((t, PROJ_WIDTH), F32),
        scratch_shapes=[pltpu.VMEM((tm, D_MODEL), BF16)],
        compiler_params=pltpu.CompilerParams(
            dimension_semantics=("parallel", "arbitrary"), vmem_limit_bytes=VMEM_LIMIT),
        name="inproj",
    )(x2, g, wp)


def _gla_kernel(qk_ref, v_ref, og_ref, gal_ref, w2_ref, ab_ref, gn_ref, y_ref, st_ref,
                qi_ref, ki_ref, qd_ref, kd_ref, dec_ref, *, n_chunks, wide):
    @pl.when(pl.program_id(1) == 0)
    def _():
        st_ref[...] = jnp.zeros_like(st_ref)

    row = _iota((CHUNK, CHUNK), 0)
    col = _iota((CHUNK, CHUNK), 1)
    causal = row >= col
    tri = causal.astype(BF16)
    w2 = w2_ref[...]
    ab = ab_ref[...]
    gn = gn_ref[...]
    heads = range(GLA_HEADS)
    ks = [slice(h * GLA_DK, (h + 1) * GLA_DK) for h in heads]
    vs = [slice(h * GLA_DV, (h + 1) * GLA_DV) for h in heads]

    group = range(wide)
    n_groups = n_chunks // wide

    def rows_of(g, j):
        return _chunk_rows(g * wide + j)

    def stage1_gate(g):
        return g, [_mm(gal_ref[rows_of(g, j), :], w2) for j in group]

    def stage1_cumsum(gate):
        g, lins = gate
        las = [-_softplus(-(lin + ab)) * (1.0 / GLA_GATE_NORM) for lin in lins]
        return g, [_cumsum_rows(tri, la) for la in las]

    def stage1_store(cs, slot):
        g, bs = cs
        for j in group:
            b = bs[j]
            rows = rows_of(g, j)
            b_mid = b[CHUNK // 2 - 1:CHUNK // 2, :]
            b_last = b[CHUNK - 1:CHUNK, :]
            q = qk_ref[rows, 0:GLA_KEY] * (GLA_DK ** -0.5)
            k = qk_ref[rows, GLA_KEY:2 * GLA_KEY]
            qi_ref[slot, j] = (q * jnp.exp(b - b_mid)).astype(BF16)
            ki_ref[slot, j] = (k * jnp.exp(b_mid - b)).astype(BF16)
            qd_ref[slot, j] = (q * jnp.exp(b)).astype(BF16)
            kd_ref[slot, j] = (k * jnp.exp(b_last - b)).astype(BF16)
            dec_ref[slot, j] = jnp.exp(b_last)

    def stage2_front(g, slot):
        v = [[v_ref[rows_of(g, j), vs[h]].astype(BF16) for h in heads] for j in group]
        st = [st_ref[h] for h in heads]
        s = [[_mm_nt(qi_ref[slot, j, :, ks[h]], ki_ref[slot, j, :, ks[h]]) for h in heads] for j in group]
        upd = [[_mm_tn(v[j][h], kd_ref[slot, j, :, ks[h]]) for h in heads] for j in group]
        inter0 = [_mm_nt(qd_ref[slot, 0, :, ks[h]], st[h]) for h in heads]
        return v, st, s, upd, inter0

    def stage2_back(g, slot, front):
        v, st, s, upd, inter0 = front
        intra = [[_mm(jnp.where(causal, s[j][h], 0.0), v[j][h]) for h in heads] for j in group]
        inter = [inter0]
        for j in group:
            st = [st[h] * dec_ref[slot, j, :, ks[h]] + upd[j][h] for h in heads]
            if j + 1 < wide:
                inter.append([_mm_nt(qd_ref[slot, j + 1, :, ks[h]], st[h]) for h in heads])
        for h in heads:
            st_ref[h] = st[h]
        for j in group:
            rows = rows_of(g, j)
            for h in heads:
                o = intra[j][h] + inter[j][h]
                o = o * lax.rsqrt(jnp.mean(o * o, axis=-1, keepdims=True) + GLA_NORM_EPS) * gn
                og = og_ref[rows, vs[h]]
                y_ref[rows, vs[h]] = (o * (og * _sigmoid(og))).astype(BF16)

    stage1_store(stage1_cumsum(stage1_gate(0)), 0)

    def body(g, carry):
        slot = g & 1
        front = stage2_front(g, slot)
        gate = stage1_gate(g + 1)
        stage2_back(g, slot, front)
        stage1_store(stage1_cumsum(gate), 1 - slot)
        return carry

    lax.fori_loop(0, n_groups - 1, body, 0)
    last_slot = (n_groups - 1) & 1
    stage2_back(n_groups - 1, last_slot, stage2_front(n_groups - 1, last_slot))


def _gla(proj, w2p, ab, gn, *, batch, seq, lblk):
    t = batch * seq
    nsb = seq // lblk
    row_map = lambda cb: (lambda b, i: (b * nsb + i, cb))
    const = lambda b, i: (0, 0)
    n_chunks = lblk // CHUNK
    wide = next(w for w in (4, 2, 1) if n_chunks % w == 0)
    hand = lambda rows, dtype: pltpu.VMEM((2, wide, rows, GLA_KEY), dtype)
    return pl.pallas_call(
        functools.partial(_gla_kernel, n_chunks=n_chunks, wide=wide),
        grid=(batch, nsb),
        in_specs=[
            pl.BlockSpec((lblk, 2 * GLA_KEY), row_map(COL_GLA_QK // (2 * GLA_KEY))),
            pl.BlockSpec((lblk, GLA_VAL), row_map(COL_GLA_V // GLA_VAL)),
            pl.BlockSpec((lblk, GLA_VAL), row_map(COL_GLA_OG // GLA_VAL)),
            pl.BlockSpec((lblk, LANES), row_map(COL_GAL // LANES)),
            pl.BlockSpec((LANES, GLA_KEY), const),
            pl.BlockSpec((1, GLA_KEY), const),
            pl.BlockSpec((1, GLA_DV), const),
        ],
        out_specs=pl.BlockSpec((lblk, GLA_VAL), lambda b, i: (b * nsb + i, 0)),
        out_shape=jax.ShapeDtypeStruct((t, GLA_VAL), BF16),
        scratch_shapes=[
            pltpu.VMEM((GLA_HEADS, GLA_DV, GLA_DK), F32),
            hand(CHUNK, BF16), hand(CHUNK, BF16), hand(CHUNK, BF16), hand(CHUNK, BF16),
            hand(1, F32),
        ],
        compiler_params=pltpu.CompilerParams(
            dimension_semantics=("parallel", "arbitrary"), vmem_limit_bytes=VMEM_LIMIT),
        name="gla",
    )(proj, proj, proj, proj, w2p, ab, gn)


def _pair_block_diag(y, lo):
    zero = jnp.zeros_like(y)
    return jnp.concatenate([jnp.where(lo, y, zero), jnp.where(lo, zero, y)], axis=0)


def _unit_lower_inverse(ns, masks, after_stage=None):
    after_stage = after_stage or {}
    stage = [0]

    def stage_done():
        hook = after_stage.get(stage[0])
        if hook is not None:
            hook()
        stage[0] += 1

    eye, blk16, off32, off64, lo = masks
    nd = [jnp.where(blk16, n, 0.0) for n in ns]
    x = [eye + d for d in nd]
    nd = [d.astype(BF16) for d in nd]
    m = [_mm(d, _pair_block_diag(d, lo)).astype(BF16) for d in nd]
    stage_done()
    for step in range(3):
        mb = [_pair_block_diag(mi, lo) for mi in m]
        xm = [_mm(xi, mbi) for xi, mbi in zip(x, mb)]
        if step < 2:
            m = [_mm(mi, mbi).astype(BF16) for mi, mbi in zip(m, mb)]
        x = [xi + xmi for xi, xmi in zip(x, xm)]
        stage_done()
    for off in (off32, off64):
        xb = [xi.astype(BF16) for xi in x]
        xn = [_mm(xi, _pair_block_diag(jnp.where(off, n, 0.0).astype(BF16), lo)) for xi, n in zip(xb, ns)]
        stage_done()
        xnx = [_mm(a, _pair_block_diag(xi, lo)) for a, xi in zip(xn, xb)]
        x = [xi + b for xi, b in zip(x, xnx)]
        stage_done()
    return x


def _rwkv_kernel(r_ref, k_ref, v_ref, lo_ref, mu_r_ref, mu_k_ref, mu_v_ref, mu_lo_ref,
                 w0_ref, a0_ref, kk_ref, ka_ref, rk_ref, lnw_ref, lnb_ref,
                 ww2_ref, aw2_ref, gw2_ref, y_ref,
                 pr_ref, pk_ref, pv_ref, plo_ref, st_ref,
                 n_ref, prm_ref, ahl_ref, bk2_ref, vx_ref, akv_ref, v2_ref, bonv_ref, gate_ref, elast_ref,
                 t_ref,
                 *, n_chunks, n_pairs):
    @pl.when(pl.program_id(2) == 0)
    def _():
        pr_ref[...] = jnp.zeros_like(pr_ref)
        pk_ref[...] = jnp.zeros_like(pk_ref)
        pv_ref[...] = jnp.zeros_like(pv_ref)
        plo_ref[...] = jnp.zeros_like(plo_ref)
        st_ref[...] = jnp.zeros_like(st_ref)

    width = n_pairs * LANES
    tri = (_iota((CHUNK, CHUNK), 0) >= _iota((CHUNK, CHUNK), 1)).astype(BF16)

    trow = _iota((CHUNK, LANES), 0)
    tcol = _iota((CHUNK, LANES), 1) & (RWKV_HEAD - 1)
    lo = _iota((CHUNK, LANES), 1) < RWKV_HEAD
    strict = trow > tcol
    incl = trow >= tcol
    blk16 = (trow // 16) == (tcol // 16)
    blk32 = (trow // 32) == (tcol // 32)
    inv_masks = ((trow == tcol).astype(F32), blk16, blk32 & (~blk16), ~blk32, lo)
    srow = _iota((LANES, LANES), 0)
    scol = _iota((LANES, LANES), 1)
    same = (srow < RWKV_HEAD) == (scol < RWKV_HEAD)
    first_w = _iota((CHUNK, width), 0) == 0
    first_lo = _iota((CHUNK, LORA_WIDTH), 0) == 0
    pairs = range(n_pairs)
    ls = [slice(p * LANES, (p + 1) * LANES) for p in pairs]

    def seg_sum(x):
        s0 = jnp.sum(jnp.where(lo, x, 0.0), axis=-1, keepdims=True)
        s1 = jnp.sum(jnp.where(lo, 0.0, x), axis=-1, keepdims=True)
        return jnp.where(lo, s0, s1)

    def stack(a, b):
        return jnp.concatenate([a, b], axis=0)

    def shifted(z, prev_ref, first):
        zs = jnp.where(first, prev_ref[...], pltpu.roll(z, 1, axis=0))
        prev_ref[...] = z[CHUNK - 1:CHUNK, :]
        return zs

    def stage1_lora(c):
        rows = _chunk_rows(c)
        zl = lo_ref[rows, :]
        lora = zl + (shifted(zl, plo_ref, first_lo) - zl) * mu_lo_ref[...]
        wa = lora[:, 0:LANES]
        gl = lora[:, LANES:LORA_WIDTH]
        return (c, _mm(jnp.tanh(wa), ww2_ref[...]), _mm(wa, aw2_ref[...]),
                _mm(_sigmoid(gl), gw2_ref[...]))

    def stage1_decay(lora_out):
        c, w_lin, a_lin, gate_all = lora_out
        logw_all = (-DECAY_SCALE) * _sigmoid(w0_ref[...] + w_lin)
        return c, logw_all, a_lin, gate_all, _cumsum_rows(tri, logw_all)

    def stage1_back(front, slot):
        c, logw_all, a_lin, gate_all, lw_all = front
        rows = _chunk_rows(c)
        zr = r_ref[rows, :]
        zk = k_ref[rows, :]
        zv = v_ref[rows, :]
        r_all = zr + (shifted(zr, pr_ref, first_w) - zr) * mu_r_ref[...]
        k_all = zk + (shifted(zk, pk_ref, first_w) - zk) * mu_k_ref[...]
        v_all = zv + (shifted(zv, pv_ref, first_w) - zv) * mu_v_ref[...]
        a_all = _sigmoid(a0_ref[...] + a_lin)

        r = [r_all[:, s] for s in ls]
        k = [k_all[:, s] for s in ls]
        v = [v_all[:, s] for s in ls]
        a = [a_all[:, s] for s in ls]
        lw = [lw_all[:, s] for s in ls]
        lwx = [lw[p] - logw_all[:, ls[p]] for p in pairs]
        kk = [k[p] * kk_ref[:, ls[p]] for p in pairs]
        kkn = [kk[p] * lax.rsqrt(jnp.maximum(seg_sum(kk[p] * kk[p]), KK_NORM_FLOOR ** 2)) for p in pairs]
        kmod = [k[p] * (1.0 + (a[p] - 1.0) * ka_ref[:, ls[p]]) for p in pairs]
        bvec = [kkn[p] * a[p] for p in pairs]
        mid = [x[CHUNK // 2 - 1:CHUNK // 2, :] for x in lw]
        last = [x[CHUNK - 1:CHUNK, :] for x in lw]
        zero = jnp.zeros((CHUNK, LANES), BF16)
        ag = [(-kkn[p] * jnp.exp(lwx[p] - mid[p])).astype(BF16) for p in pairs]
        rg = [(r[p] * jnp.exp(lw[p] - mid[p])).astype(BF16) for p in pairs]
        e_ng = [jnp.exp(mid[p] - lw[p]) for p in pairs]
        bg = [(bvec[p] * e_ng[p]).astype(BF16) for p in pairs]
        kg = [(kmod[p] * e_ng[p]).astype(BF16) for p in pairs]
        v16 = [v[p].astype(BF16) for p in pairs]
        pm0 = [_mm_nt(stack(jnp.where(lo, ag[p], zero), jnp.where(lo, rg[p], zero)), stack(bg[p], kg[p]))
               for p in pairs]
        pm1 = [_mm_nt(stack(jnp.where(lo, zero, ag[p]), jnp.where(lo, zero, rg[p])), stack(kg[p], bg[p]))
               for p in pairs]
        ak = [jnp.where(strict, jnp.where(lo, pm1[p][0:CHUNK], pm0[p][0:CHUNK]), 0.0) for p in pairs]
        vx = [stack(jnp.where(lo, zero, v16[p]), jnp.where(lo, v16[p], zero)) for p in pairs]
        akv = [_mm(ak[p], vx[p]) for p in pairs]
        for p in pairs:
            g_ls = jnp.exp(last[p] - lw[p])
            n_ref[slot, p] = jnp.where(strict, jnp.where(lo, pm0[p][0:CHUNK], pm1[p][0:CHUNK]), 0.0)
            rb = jnp.where(incl, jnp.where(lo, pm0[p][CHUNK:], pm1[p][CHUNK:]), 0.0)
            rk = jnp.where(incl, jnp.where(lo, pm1[p][CHUNK:], pm0[p][CHUNK:]), 0.0)
            prm_ref[slot, p] = jnp.concatenate([rb, rk], axis=1).astype(BF16)
            vx_ref[slot, p] = vx[p]
            akv_ref[slot, p] = akv[p]
            ahl_ref[slot, p] = stack(-kkn[p] * jnp.exp(lwx[p]), r[p] * jnp.exp(lw[p])).astype(BF16)
            bk2_ref[slot, p] = stack(bvec[p] * g_ls, kmod[p] * g_ls).astype(BF16)
            v2_ref[slot, p] = v16[p]
            bonv_ref[slot, p] = seg_sum(r[p] * kmod[p] * rk_ref[:, ls[p]]) * v[p]
            gate_ref[slot, p] = gate_all[:, ls[p]]
            elast_ref[slot, p] = jnp.exp(last[p])

    def stage2(slot, tslot, after_stage=None):
        tinv = _unit_lower_inverse([n_ref[slot, p] for p in pairs], inv_masks, after_stage)
        for p in pairs:
            t_ref[tslot, p] = tinv[p].astype(BF16)

    def stage3_state(slot):
        st = [st_ref[p] for p in pairs]
        ah = [_mm_nt(ahl_ref[slot, p], st[p]) for p in pairs]
        return st, ah

    def stage3_solve(slot, tslot, front):
        st, ah = front
        w = [(ah[p][0:CHUNK] + akv_ref[slot, p]).astype(BF16) for p in pairs]
        return [_mm(t_ref[tslot, p], _pair_block_diag(w[p], lo)).astype(BF16) for p in pairs]

    def stage3_out(c, slot, front, u):
        st, ah = front
        rows = _chunk_rows(c)
        ys = [_mm(prm_ref[slot, p], stack(_pair_block_diag(u[p], lo), vx_ref[slot, p])) for p in pairs]
        dh = [_mm_tn(stack(u[p], v2_ref[slot, p]), bk2_ref[slot, p]) for p in pairs]
        for p in pairs:
            st_ref[p] = st[p] * elast_ref[slot, p] + jnp.where(same, dh[p], 0.0)
            y = ah[p][CHUNK:] + ys[p]
            mean = seg_sum(y) * (1.0 / RWKV_HEAD)
            d = y - mean
            var = seg_sum(d * d) * (1.0 / RWKV_HEAD)
            yn = d * lax.rsqrt(var + RWKV_GN_EPS) * lnw_ref[:, ls[p]] + lnb_ref[:, ls[p]]
            y_ref[rows, ls[p]] = ((yn + bonv_ref[slot, p]) * gate_ref[slot, p]).astype(BF16)

    def step(c):
        s1, s2, s3 = [0 <= c + k < n_chunks if isinstance(c, int) else True for k in (2, 1, 0)]
        slot3, slot2, slot1 = c % 3, (c + 1) % 3, (c + 2) % 3
        tslot3, tslot2 = c & 1, (c + 1) & 1
        held = {}

        def after0():
            if s1:
                held["lora"] = stage1_lora(c + 2)
            if s3:
                held["u"] = stage3_solve(slot3, tslot3, held["state"])

        def after1():
            if s3:
                stage3_out(c, slot3, held["state"], held["u"])

        def after2():
            if s1:
                held["front"] = stage1_decay(held["lora"])

        if s3:
            held["state"] = stage3_state(slot3)
        if s2:
            stage2(slot2, tslot2, {0: after0, 1: after1, 2: after2})
        else:
            after0(), after1(), after2()
        if s1:
            stage1_back(held["front"], slot1)

    n_steady = max(n_chunks - 2, 0)
    for c in range(-2, 0):
        step(c)
    if n_steady > 0:
        def body(c, carry):
            step(c)
            return carry

        lax.fori_loop(0, n_steady, body, 0)
    for c in range(n_steady, n_chunks):
        step(c)


def _rwkv(proj, params, *, batch, seq, lblk, n_pairs):
    (mu_r, mu_k, mu_v, mu_lo, w0, a0, k_k, k_a, r_k, ln_w, ln_b, ww2p, aw2p, gw2) = params
    t = batch * seq
    nsb = seq // lblk
    width = n_pairs * LANES
    ngrp = RWKV_DIM // width
    row_map = lambda c0: (lambda b, g, i: (b * nsb + i, c0 // width + g))
    grp = lambda b, g, i: (0, g)
    const = lambda b, g, i: (0, 0)
    vec = pl.BlockSpec((1, width), grp)
    rec = lambda slots, rows, cols, dtype: pltpu.VMEM((slots, n_pairs, rows, cols), dtype)
    return pl.pallas_call(
        functools.partial(_rwkv_kernel, n_chunks=lblk // CHUNK, n_pairs=n_pairs),
        grid=(batch, ngrp, nsb),
        in_specs=[
            pl.BlockSpec((lblk, width), row_map(COL_R)),
            pl.BlockSpec((lblk, width), row_map(COL_K)),
            pl.BlockSpec((lblk, width), row_map(COL_V)),
            pl.BlockSpec((lblk, LORA_WIDTH), lambda b, g, i: (b * nsb + i, COL_LORA // LORA_WIDTH)),
            vec, vec, vec,
            pl.BlockSpec((1, LORA_WIDTH), const),
            vec, vec, vec, vec, vec, vec, vec,
            pl.BlockSpec((LANES, width), grp),
            pl.BlockSpec((LANES, width), grp),
            pl.BlockSpec((GATE_LORA, width), grp),
        ],
        out_specs=pl.BlockSpec((lblk, width), lambda b, g, i: (b * nsb + i, g)),
        out_shape=jax.ShapeDtypeStruct((t, RWKV_DIM), BF16),
        scratch_shapes=[
            pltpu.VMEM((1, width), F32), pltpu.VMEM((1, width), F32), pltpu.VMEM((1, width), F32),
            pltpu.VMEM((1, LORA_WIDTH), F32),
            pltpu.VMEM((n_pairs, LANES, LANES), F32),
            rec(3, CHUNK, LANES, F32),
            rec(3, CHUNK, 2 * LANES, BF16),
            rec(3, LANES, LANES, BF16),
            rec(3, LANES, LANES, BF16),
            rec(3, LANES, LANES, BF16),
            rec(3, CHUNK, LANES, F32),
            rec(3, CHUNK, LANES, BF16),
            rec(3, CHUNK, LANES, F32),
            rec(3, CHUNK, LANES, F32),
            rec(3, 1, LANES, F32),
            rec(2, CHUNK, LANES, BF16),
        ],
        compiler_params=pltpu.CompilerParams(
            dimension_semantics=("parallel", "parallel", "arbitrary"), vmem_limit_bytes=VMEM_LIMIT),
        name="rwkv",
    )(proj, proj, proj, proj, mu_r, mu_k, mu_v, mu_lo, w0, a0, k_k, k_a, r_k, ln_w, ln_b,
      ww2p, aw2p, gw2)


def _merge_kernel(x_ref, yg_ref, yr_ref, ga_ref, gb_ref, wg_ref, wr_ref, wo_ref, o_ref):
    mixed = (_sigmoid(ga_ref[...]) * jnp.dot(yg_ref[...], wg_ref[...], preferred_element_type=F32)
             + _sigmoid(gb_ref[...]) * jnp.dot(yr_ref[...], wr_ref[...], preferred_element_type=F32))
    o_ref[...] = x_ref[...] + _mm(mixed, wo_ref[...])


def _merge(x2, yg, yr, proj, wg, wr, wo, *, tm):
    t = x2.shape[0]
    tok = lambda i: (i, 0)
    const = lambda i: (0, 0)
    wspec = pl.BlockSpec((D_MODEL, D_MODEL), const)
    return pl.pallas_call(
        _merge_kernel,
        grid=(t // tm,),
        in_specs=[
            pl.BlockSpec((tm, D_MODEL), tok),
            pl.BlockSpec((tm, D_MODEL), tok),
            pl.BlockSpec((tm, D_MODEL), tok),
            pl.BlockSpec((tm, D_MODEL), lambda i: (i, COL_GATE // D_MODEL)),
            pl.BlockSpec((tm, D_MODEL), lambda i: (i, COL_GATE // D_MODEL + 1)),
            wspec, wspec, wspec,
        ],
        out_specs=pl.BlockSpec((tm, D_MODEL), tok),
        out_shape=jax.ShapeDtypeStruct((t, D_MODEL), F32),
        compiler_params=pltpu.CompilerParams(
            dimension_semantics=("parallel",), vmem_limit_bytes=VMEM_LIMIT),
        name="merge",
    )(x2, yg, yr, proj, proj, wg, wr, wo)


def _mlp_kernel(x_ref, g_ref, wu_ref, wd_ref, gf_ref, o_ref, *, final_norm):
    x = x_ref[...]
    h = (x * lax.rsqrt(jnp.mean(x * x, axis=-1, keepdims=True) + NORM_EPS) * g_ref[...]).astype(BF16)
    up = jnp.dot(h, wu_ref[...], preferred_element_type=F32)
    act = jnp.square(jnp.maximum(up, 0.0)).astype(BF16)
    x = x + jnp.dot(act, wd_ref[...], preferred_element_type=F32)
    if final_norm:
        x = x * lax.rsqrt(jnp.mean(x * x, axis=-1, keepdims=True) + NORM_EPS) * gf_ref[...]
    o_ref[...] = x


def _mlp(x2, g, wu, wd, gf, *, tm, final_norm):
    t = x2.shape[0]
    tok = lambda i: (i, 0)
    const = lambda i: (0, 0)
    return pl.pallas_call(
        functools.partial(_mlp_kernel, final_norm=final_norm),
        grid=(t // tm,),
        in_specs=[
            pl.BlockSpec((tm, D_MODEL), tok),
            pl.BlockSpec((1, D_MODEL), const),
            pl.BlockSpec((D_MODEL, D_FF), const),
            pl.BlockSpec((D_FF, D_MODEL), const),
            pl.BlockSpec((1, D_MODEL), const),
        ],
        out_specs=pl.BlockSpec((tm, D_MODEL), tok),
        out_shape=jax.ShapeDtypeStruct((t, D_MODEL), F32),
        compiler_params=pltpu.CompilerParams(
            dimension_semantics=("parallel",), vmem_limit_bytes=VMEM_LIMIT),
        name="mlp",
    )(x2, g, wu, wd, gf)


def _regroup_w_in(w):
    w = w.astype(BF16)
    gla_w = w[:, :GLA_WIDTH]
    rw = w[:, GLA_WIDTH:GLA_WIDTH + RWKV_WIDTH]
    gates = w[:, GLA_WIDTH + RWKV_WIDTH:]
    gla_main = GLA_WIDTH - GLA_GATE_RANK
    pad = jnp.zeros((w.shape[0], PROJ_WIDTH - COL_GAL - GLA_GATE_RANK), w.dtype)
    return jnp.concatenate([gates, gla_w[:, :gla_main], rw, gla_w[:, gla_main:], pad], axis=1)


def _row(v):
    return v.reshape(1, -1).astype(F32)


def kernel(x, norm_mix, w_in, gla_a_w2, gla_a_b, gla_norm, rwkv_mu, rwkv_w0, rwkv_w_w2, rwkv_a0, rwkv_a_w2, rwkv_g_w2, rwkv_k_k, rwkv_k_a, rwkv_r_k, rwkv_ln_w, rwkv_ln_b, w_branch_gla, w_branch_rwkv, w_out, norm_mlp, w_up, w_down, norm_final):
    batch, seq, d = x.shape
    assert d == D_MODEL and seq % CHUNK == 0
    t = batch * seq
    depth = norm_mix.shape[0]
    tm = 512 if t % 512 == 0 else CHUNK
    lblk = 1024 if seq % 1024 == 0 else CHUNK
    x2 = x.reshape(t, d)
    for l in range(depth):
        tn = PROJ_WIDTH // 4
        wp = _regroup_w_in(w_in[l]).reshape(D_MODEL, PROJ_WIDTH // tn, tn).transpose(1, 0, 2)
        proj = _inproj(x2, _row(norm_mix[l]), wp, tm=(1024 if t % 1024 == 0 else tm), tn=tn)

        w2p = jnp.zeros((LANES, GLA_KEY), BF16).at[:GLA_GATE_RANK].set(gla_a_w2[l].astype(BF16))
        y_gla = _gla(proj, w2p, _row(gla_a_b[l]), _row(gla_norm[l]), batch=batch, seq=seq, lblk=lblk)

        mu = rwkv_mu[l]
        zeros = jnp.zeros((DECAY_LORA, RWKV_DIM), BF16)
        ww2p = jnp.concatenate([rwkv_w_w2[l].astype(BF16), zeros], axis=0)
        aw2p = jnp.concatenate([zeros, rwkv_a_w2[l].astype(BF16)], axis=0)
        params = (_row(mu[0:RWKV_DIM]), _row(mu[RWKV_DIM:2 * RWKV_DIM]), _row(mu[2 * RWKV_DIM:3 * RWKV_DIM]),
                  _row(mu[3 * RWKV_DIM:]), _row(rwkv_w0[l]), _row(rwkv_a0[l]), _row(rwkv_k_k[l]),
                  _row(rwkv_k_a[l]), _row(rwkv_r_k[l]), _row(rwkv_ln_w[l]), _row(rwkv_ln_b[l]),
                  ww2p, aw2p, rwkv_g_w2[l].astype(BF16))
        y_rwkv = _rwkv(proj, params, batch=batch, seq=seq, lblk=lblk, n_pairs=RWKV_DIM // LANES)

        x2 = _merge(x2, y_gla, y_rwkv, proj, w_branch_gla[l].astype(BF16), w_branch_rwkv[l].astype(BF16),
                    w_out[l].astype(BF16), tm=tm)
        x2 = _mlp(x2, _row(norm_mlp[l]), w_up[l].astype(BF16), w_down[l].astype(BF16), _row(norm_final),
                  tm=tm, final_norm=(l == depth - 1))
    return x2.reshape(batch, seq, d)
```

```python
import functools

import jax
import jax.numpy as jnp
from jax import lax
from jax.experimental import pallas as pl
from jax.experimental.pallas import tpu as pltpu

F32 = jnp.float32
BF16 = jnp.bfloat16

D_MODEL = 1024
GLA_HEADS = 4
GLA_KEY = D_MODEL // 2
GLA_VAL = D_MODEL
GLA_DK = GLA_KEY // GLA_HEADS
GLA_DV = GLA_VAL // GLA_HEADS
GLA_GATE_RANK = 16
GLA_GATE_NORM = 16.0
GLA_NORM_EPS = 1e-5
RWKV_HEAD = 64
RWKV_DIM = D_MODEL
DECAY_LORA = 64
AAA_LORA = 64
GATE_LORA = 128
RWKV_GN_EPS = 64e-5
KK_NORM_FLOOR = 1e-12
DECAY_SCALE = 0.6065306597126334
D_FF = 4 * D_MODEL
NORM_EPS = 1e-6

GLA_WIDTH = 2 * GLA_KEY + 2 * GLA_VAL + GLA_GATE_RANK
RWKV_WIDTH = 3 * RWKV_DIM + DECAY_LORA + AAA_LORA + GATE_LORA

LANES = 128
CHUNK = 64
COL_GATE = 0
COL_GLA_QK = 2048
COL_GLA_V = 3072
COL_GLA_OG = 4096
COL_R = 5120
COL_K = 6144
COL_V = 7168
COL_LORA = 8192
COL_GAL = 8448
PROJ_WIDTH = 8704
LORA_WIDTH = DECAY_LORA + AAA_LORA + GATE_LORA

VMEM_LIMIT = 56 * 1024 * 1024


def _mm(a, b):
    return jnp.dot(a.astype(BF16), b.astype(BF16), preferred_element_type=F32)


def _mm_nt(a, b):
    return lax.dot_general(a.astype(BF16), b.astype(BF16), (((1,), (1,)), ((), ())),
                           preferred_element_type=F32)


def _mm_tn(a, b):
    return lax.dot_general(a.astype(BF16), b.astype(BF16), (((0,), (0,)), ((), ())),
                           preferred_element_type=F32)


def _cumsum_rows(tri, x):
    tri = tri.astype(BF16)
    hi = x.astype(BF16)
    lo = (x - hi.astype(F32)).astype(BF16)
    return (jnp.dot(tri, hi, preferred_element_type=F32) + jnp.dot(tri, lo, preferred_element_type=F32))


def _softplus(x):
    return jnp.maximum(x, 0.0) + jnp.log(1.0 + jnp.exp(-jnp.abs(x)))


def _sigmoid(x):
    return 1.0 / (1.0 + jnp.exp(-x))


def _iota(shape, dim):
    return lax.broadcasted_iota(jnp.int32, shape, dim)


def _chunk_rows(c):
    if isinstance(c, int):
        return pl.ds(c * CHUNK, CHUNK)
    return pl.ds(pl.multiple_of(c * CHUNK, CHUNK), CHUNK)


def _inproj_kernel(x_ref, g_ref, w_ref, o_ref, h_ref):
    @pl.when(pl.program_id(1) == 0)
    def _():
        x = x_ref[...]
        ms = jnp.mean(x * x, axis=-1, keepdims=True)
        h_ref[...] = (x * lax.rsqrt(ms + NORM_EPS) * g_ref[...]).astype(BF16)

    o_ref[...] = jnp.dot(h_ref[...], w_ref[pl.program_id(1)], preferred_element_type=F32)


def _inproj(x2, g, wp, *, tm, tn):
    t = x2.shape[0]
    return pl.pallas_call(
        _inproj_kernel,
        grid=(t // tm, PROJ_WIDTH // tn),
        in_specs=[
            pl.BlockSpec((tm, D_MODEL), lambda i, j: (i, 0)),
            pl.BlockSpec((1, D_MODEL), lambda i, j: (0, 0)),
            pl.BlockSpec((PROJ_WIDTH // tn, D_MODEL, tn), lambda i, j: (0, 0, 0),
                         pipeline_mode=pl.Buffered(1)),
        ],
        out_specs=pl.BlockSpec((tm, tn), lambda i, j: (i, j)),
        out_shape=jax.ShapeDtypeStruct((t, PROJ_WIDTH), F32),
        scratch_shapes=[pltpu.VMEM((tm, D_MODEL), BF16)],
        compiler_params=pltpu.CompilerParams(
            dimension_semantics=("parallel", "arbitrary"), vmem_limit_bytes=VMEM_LIMIT),
        name="inproj",
    )(x2, g, wp)


def _gla_kernel(qk_ref, v_ref, og_ref, gal_ref, w2_ref, ab_ref, gn_ref, y_ref, st_ref,
                qi_ref, ki_ref, qd_ref, kd_ref, dec_ref, *, n_chunks, wide):
    @pl.when(pl.program_id(1) == 0)
    def _():
        st_ref[...] = jnp.zeros_like(st_ref)

    row = _iota((CHUNK, CHUNK), 0)
    col = _iota((CHUNK, CHUNK), 1)
    causal = row >= col
    tri = causal.astype(BF16)
    w2 = w2_ref[...]
    ab = ab_ref[...]
    gn = gn_ref[...]
    heads = range(GLA_HEADS)
    ks = [slice(h * GLA_DK, (h + 1) * GLA_DK) for h in heads]
    vs = [slice(h * GLA_DV, (h + 1) * GLA_DV) for h in heads]

    group = range(wide)
    n_groups = n_chunks // wide

    def rows_of(g, j):
        return _chunk_rows(g * wide + j)

    def stage1_gate(g):
        return g, [_mm(gal_ref[rows_of(g, j), :], w2) for j in group]

    def stage1_cumsum(gate):
        g, lins = gate
        las = [-_softplus(-(lin + ab)) * (1.0 / GLA_GATE_NORM) for lin in lins]
        return g, [_cumsum_rows(tri, la) for la in las]

    def stage1_store(cs, slot):
        g, bs = cs
        for j in group:
            b = bs[j]
            rows = rows_of(g, j)
            b_mid = b[CHUNK // 2 - 1:CHUNK // 2, :]
            b_last = b[CHUNK - 1:CHUNK, :]
            q = qk_ref[rows, 0:GLA_KEY] * (GLA_DK ** -0.5)
            k = qk_ref[rows, GLA_KEY:2 * GLA_KEY]
            e_in = jnp.exp(b - b_mid)
            qi = q * e_in
            ki = k * (1.0 / e_in)
            qi_ref[slot, j] = qi.astype(BF16)
            ki_ref[slot, j] = ki.astype(BF16)
            qd_ref[slot, j] = (qi * jnp.exp(b_mid)).astype(BF16)
            kd_ref[slot, j] = (ki * jnp.exp(b_last - b_mid)).astype(BF16)
            dec_ref[slot, j] = jnp.exp(b_last)

    def stage2_front(g, slot):
        v = [[v_ref[rows_of(g, j), vs[h]].astype(BF16) for h in heads] for j in group]
        st = [st_ref[h] for h in heads]
        s = [[_mm_nt(qi_ref[slot, j, :, ks[h]], ki_ref[slot, j, :, ks[h]]) for h in heads] for j in group]
        upd = [[_mm_tn(v[j][h], kd_ref[slot, j, :, ks[h]]) for h in heads] for j in group]
        inter0 = [_mm_nt(qd_ref[slot, 0, :, ks[h]], st[h]) for h in heads]
        return v, st, s, upd, inter0

    def stage2_back(g, slot, front):
        v, st, s, upd, inter0 = front
        intra = [[_mm(jnp.where(causal, s[j][h], 0.0), v[j][h]) for h in heads] for j in group]
        inter = [inter0]
        for j in group:
            st = [st[h] * dec_ref[slot, j, :, ks[h]] + upd[j][h] for h in heads]
            if j + 1 < wide:
                inter.append([_mm_nt(qd_ref[slot, j + 1, :, ks[h]], st[h]) for h in heads])
        for h in heads:
            st_ref[h] = st[h]
        for j in group:
            rows = rows_of(g, j)
            for h in heads:
                o = intra[j][h] + inter[j][h]
                o = o * lax.rsqrt(jnp.mean(o * o, axis=-1, keepdims=True) + GLA_NORM_EPS) * gn
                og = og_ref[rows, vs[h]]
                y_ref[rows, vs[h]] = (o * (og * _sigmoid(og))).astype(BF16)

    stage1_store(stage1_cumsum(stage1_gate(0)), 0)

    def body(g, carry):
        slot = g & 1
        front = stage2_front(g, slot)
        gate = stage1_gate(g + 1)
        stage2_back(g, slot, front)
        stage1_store(stage1_cumsum(gate), 1 - slot)
        return carry

    lax.fori_loop(0, n_groups - 1, body, 0)
    last_slot = (n_groups - 1) & 1
    stage2_back(n_groups - 1, last_slot, stage2_front(n_groups - 1, last_slot))


def _gla(proj, w2p, ab, gn, *, batch, seq, lblk):
    t = batch * seq
    nsb = seq // lblk
    row_map = lambda cb: (lambda b, i: (b * nsb + i, cb))
    const = lambda b, i: (0, 0)
    n_chunks = lblk // CHUNK
    wide = next(w for w in (4, 2, 1) if n_chunks % w == 0)
    hand = lambda rows, dtype: pltpu.VMEM((2, wide, rows, GLA_KEY), dtype)
    return pl.pallas_call(
        functools.partial(_gla_kernel, n_chunks=n_chunks, wide=wide),
        grid=(batch, nsb),
        in_specs=[
            pl.BlockSpec((lblk, 2 * GLA_KEY), row_map(COL_GLA_QK // (2 * GLA_KEY))),
            pl.BlockSpec((lblk, GLA_VAL), row_map(COL_GLA_V // GLA_VAL)),
            pl.BlockSpec((lblk, GLA_VAL), row_map(COL_GLA_OG // GLA_VAL)),
            pl.BlockSpec((lblk, LANES), row_map(COL_GAL // LANES)),
            pl.BlockSpec((LANES, GLA_KEY), const),
            pl.BlockSpec((1, GLA_KEY), const),
            pl.BlockSpec((1, GLA_DV), const),
        ],
        out_specs=pl.BlockSpec((lblk, GLA_VAL), lambda b, i: (b * nsb + i, 0)),
        out_shape=jax.ShapeDtypeStruct((t, GLA_VAL), BF16),
        scratch_shapes=[
            pltpu.VMEM((GLA_HEADS, GLA_DV, GLA_DK), F32),
            hand(CHUNK, BF16), hand(CHUNK, BF16), hand(CHUNK, BF16), hand(CHUNK, BF16),
            hand(1, F32),
        ],
        compiler_params=pltpu.CompilerParams(
            dimension_semantics=("parallel", "arbitrary"), vmem_limit_bytes=VMEM_LIMIT),
        name="gla",
    )(proj, proj, proj, proj, w2p, ab, gn)


def _pair_block_diag(y, lo):
    zero = jnp.zeros_like(y)
    return jnp.concatenate([jnp.where(lo, y, zero), jnp.where(lo, zero, y)], axis=0)


def _unit_lower_inverse(ns, masks, after_stage=None):
    after_stage = after_stage or {}
    stage = [0]

    def stage_done():
        hook = after_stage.get(stage[0])
        if hook is not None:
            hook()
        stage[0] += 1

    eye, blk16, off32, off64, lo = masks
    nd = [jnp.where(blk16, n, 0.0) for n in ns]
    x = [eye + d for d in nd]
    nd = [d.astype(BF16) for d in nd]
    m = [_mm(d, _pair_block_diag(d, lo)).astype(BF16) for d in nd]
    stage_done()
    c = ns[0].shape[0]
    for step in range(3):
        mb = [_pair_block_diag(mi, lo) for mi in m]
        if step < 2:
            xm = [_mm(jnp.concatenate([xi.astype(BF16), mi], axis=0), mbi) for xi, mi, mbi in zip(x, m, mb)]
            m = [r[c:].astype(BF16) for r in xm]
            xm = [r[:c] for r in xm]
        else:
            xm = [_mm(xi, mbi) for xi, mbi in zip(x, mb)]
        x = [xi + xmi for xi, xmi in zip(x, xm)]
        stage_done()
    for off in (off32, off64):
        xb = [xi.astype(BF16) for xi in x]
        xn = [_mm(xi, _pair_block_diag(jnp.where(off, n, 0.0).astype(BF16), lo)) for xi, n in zip(xb, ns)]
        stage_done()
        xnx = [_mm(a, _pair_block_diag(xi, lo)) for a, xi in zip(xn, xb)]
        x = [xi + b for xi, b in zip(x, xnx)]
        stage_done()
    return x


def _rwkv_kernel(r_ref, k_ref, v_ref, lo_ref, mu_r_ref, mu_k_ref, mu_v_ref, mu_lo_ref,
                 w0_ref, a0_ref, kk_ref, ka_ref, rk_ref, lnw_ref, lnb_ref,
                 ww2_ref, aw2_ref, gw2_ref, y_ref,
                 pr_ref, pk_ref, pv_ref, plo_ref, st_ref,
                 n_ref, prm_ref, ahl_ref, bk2_ref, vx_ref, akv_ref, v2_ref, bonv_ref, gate_ref, elast_ref,
                 t_ref,
                 *, n_chunks, n_pairs):
    @pl.when(pl.program_id(2) == 0)
    def _():
        pr_ref[...] = jnp.zeros_like(pr_ref)
        pk_ref[...] = jnp.zeros_like(pk_ref)
        pv_ref[...] = jnp.zeros_like(pv_ref)
        plo_ref[...] = jnp.zeros_like(plo_ref)
        st_ref[...] = jnp.zeros_like(st_ref)

    width = n_pairs * LANES
    tri = (_iota((CHUNK, CHUNK), 0) >= _iota((CHUNK, CHUNK), 1)).astype(BF16)

    trow = _iota((CHUNK, LANES), 0)
    tcol = _iota((CHUNK, LANES), 1) & (RWKV_HEAD - 1)
    lo = _iota((CHUNK, LANES), 1) < RWKV_HEAD
    strict = trow > tcol
    incl = trow >= tcol
    blk16 = (trow // 16) == (tcol // 16)
    blk32 = (trow // 32) == (tcol // 32)
    inv_masks = ((trow == tcol).astype(F32), blk16, blk32 & (~blk16), ~blk32, lo)
    srow = _iota((LANES, LANES), 0)
    scol = _iota((LANES, LANES), 1)
    same = (srow < RWKV_HEAD) == (scol < RWKV_HEAD)
    first_w = _iota((CHUNK, width), 0) == 0
    first_lo = _iota((CHUNK, LORA_WIDTH), 0) == 0
    pairs = range(n_pairs)
    ls = [slice(p * LANES, (p + 1) * LANES) for p in pairs]

    def seg_sum(x):
        s0 = jnp.sum(jnp.where(lo, x, 0.0), axis=-1, keepdims=True)
        s1 = jnp.sum(jnp.where(lo, 0.0, x), axis=-1, keepdims=True)
        return jnp.where(lo, s0, s1)

    def stack(a, b):
        return jnp.concatenate([a, b], axis=0)

    def shifted(z, prev_ref, first):
        zs = jnp.where(first, prev_ref[...], pltpu.roll(z, 1, axis=0))
        prev_ref[...] = z[CHUNK - 1:CHUNK, :]
        return zs

    def stage1_lora(c):
        rows = _chunk_rows(c)
        zl = lo_ref[rows, :]
        lora = zl + (shifted(zl, plo_ref, first_lo) - zl) * mu_lo_ref[...]
        wl = lora[:, 0:DECAY_LORA]
        al = lora[:, DECAY_LORA:DECAY_LORA + AAA_LORA]
        gl = lora[:, DECAY_LORA + AAA_LORA:LORA_WIDTH]
        return (c, _mm(jnp.tanh(wl), ww2_ref[...]), _mm(al, aw2_ref[...]),
                _mm(_sigmoid(gl), gw2_ref[...]))

    def stage1_decay(lora_out):
        c, w_lin, a_lin, gate_all = lora_out
        logw_all = (-DECAY_SCALE) * _sigmoid(w0_ref[...] + w_lin)
        return c, logw_all, a_lin, gate_all, _cumsum_rows(tri, logw_all)

    def stage1_back(front, slot):
        c, logw_all, a_lin, gate_all, lw_all = front
        rows = _chunk_rows(c)
        zr = r_ref[rows, :]
        zk = k_ref[rows, :]
        zv = v_ref[rows, :]
        r_all = zr + (shifted(zr, pr_ref, first_w) - zr) * mu_r_ref[...]
        k_all = zk + (shifted(zk, pk_ref, first_w) - zk) * mu_k_ref[...]
        v_all = zv + (shifted(zv, pv_ref, first_w) - zv) * mu_v_ref[...]
        a_all = _sigmoid(a0_ref[...] + a_lin)

        r = [r_all[:, s] for s in ls]
        k = [k_all[:, s] for s in ls]
        v = [v_all[:, s] for s in ls]
        a = [a_all[:, s] for s in ls]
        lw = [lw_all[:, s] for s in ls]
        kk = [k[p] * kk_ref[:, ls[p]] for p in pairs]
        kkn = [kk[p] * lax.rsqrt(jnp.maximum(seg_sum(kk[p] * kk[p]), KK_NORM_FLOOR ** 2)) for p in pairs]
        kmod = [k[p] * (1.0 + (a[p] - 1.0) * ka_ref[:, ls[p]]) for p in pairs]
        bvec = [kkn[p] * a[p] for p in pairs]
        mid = [x[CHUNK // 2 - 1:CHUNK // 2, :] for x in lw]
        last = [x[CHUNK - 1:CHUNK, :] for x in lw]
        zero = jnp.zeros((CHUNK, LANES), BF16)
        e_in = [jnp.exp(lw[p] - mid[p]) for p in pairs]
        e_ng = [1.0 / e_in[p] for p in pairs]
        ag_f = [-kkn[p] * (e_in[p] * jnp.exp(-logw_all[:, ls[p]])) for p in pairs]
        rg_f = [r[p] * e_in[p] for p in pairs]
        bg_f = [bvec[p] * e_ng[p] for p in pairs]
        kg_f = [kmod[p] * e_ng[p] for p in pairs]
        ag = [x.astype(BF16) for x in ag_f]
        rg = [x.astype(BF16) for x in rg_f]
        bg = [x.astype(BF16) for x in bg_f]
        kg = [x.astype(BF16) for x in kg_f]
        v16 = [v[p].astype(BF16) for p in pairs]
        pm = [_mm_nt(jnp.concatenate([jnp.where(lo, ag[p], zero), jnp.where(lo, rg[p], zero),
                                      jnp.where(lo, zero, ag[p]), jnp.where(lo, zero, rg[p])], axis=0),
                     stack(bg[p], kg[p])) for p in pairs]
        pm0 = [x[0:2 * CHUNK] for x in pm]
        pm1 = [pltpu.roll(x[2 * CHUNK:4 * CHUNK], RWKV_HEAD, axis=1) for x in pm]
        ak = [jnp.where(strict, jnp.where(lo, pm1[p][0:CHUNK], pm0[p][0:CHUNK]), 0.0) for p in pairs]
        vx = [stack(jnp.where(lo, zero, v16[p]), jnp.where(lo, v16[p], zero)) for p in pairs]
        akv = [_mm(ak[p], vx[p]) for p in pairs]
        for p in pairs:
            e_mid = jnp.exp(mid[p])
            e_tail = jnp.exp(last[p] - mid[p])
            n_ref[slot, p] = jnp.where(strict, jnp.where(lo, pm0[p][0:CHUNK], pm1[p][0:CHUNK]), 0.0)
            rb = jnp.where(incl, jnp.where(lo, pm0[p][CHUNK:], pm1[p][CHUNK:]), 0.0)
            rk = jnp.where(incl, jnp.where(lo, pm1[p][CHUNK:], pm0[p][CHUNK:]), 0.0)
            prm_ref[slot, p] = jnp.concatenate([rb, rk], axis=1).astype(BF16)
            vx_ref[slot, p] = vx[p]
            akv_ref[slot, p] = akv[p]
            ahl_ref[slot, p] = stack((ag_f[p] * e_mid).astype(BF16), (rg_f[p] * e_mid).astype(BF16))
            bk2_ref[slot, p] = stack((bg_f[p] * e_tail).astype(BF16), (kg_f[p] * e_tail).astype(BF16))
            v2_ref[slot, p] = v16[p]
            bonv_ref[slot, p] = seg_sum(r[p] * kmod[p] * rk_ref[:, ls[p]]) * v[p]
            gate_ref[slot, p] = gate_all[:, ls[p]]
            elast_ref[slot, p] = jnp.exp(last[p])

    def stage2(slot, tslot, after_stage=None):
        tinv = _unit_lower_inverse([n_ref[slot, p] for p in pairs], inv_masks, after_stage)
        for p in pairs:
            t_ref[tslot, p] = tinv[p].astype(BF16)

    def stage3_state(slot):
        st = [st_ref[p] for p in pairs]
        ah = [_mm_nt(ahl_ref[slot, p], st[p]) for p in pairs]
        return st, ah

    def stage3_solve(slot, tslot, front):
        st, ah = front
        w = [(ah[p][0:CHUNK] + akv_ref[slot, p]).astype(BF16) for p in pairs]
        return [_mm(t_ref[tslot, p], _pair_block_diag(w[p], lo)).astype(BF16) for p in pairs]

    def stage3_out(c, slot, front, u):
        st, ah = front
        rows = _chunk_rows(c)
        ys = [_mm(prm_ref[slot, p], stack(_pair_block_diag(u[p], lo), vx_ref[slot, p])) for p in pairs]
        dh = [_mm_tn(stack(u[p], v2_ref[slot, p]), bk2_ref[slot, p]) for p in pairs]
        for p in pairs:
            st_ref[p] = st[p] * elast_ref[slot, p] + jnp.where(same, dh[p], 0.0)
            y = ah[p][CHUNK:] + ys[p]
            mean = seg_sum(y) * (1.0 / RWKV_HEAD)
            d = y - mean
            var = seg_sum(d * d) * (1.0 / RWKV_HEAD)
            yn = d * lax.rsqrt(var + RWKV_GN_EPS) * lnw_ref[:, ls[p]] + lnb_ref[:, ls[p]]
            y_ref[rows, ls[p]] = ((yn + bonv_ref[slot, p]) * gate_ref[slot, p]).astype(BF16)

    def step(c):
        s1, s2, s3 = [0 <= c + k < n_chunks if isinstance(c, int) else True for k in (2, 1, 0)]
        slot3, slot2, slot1 = c % 3, (c + 1) % 3, (c + 2) % 3
        tslot3, tslot2 = c & 1, (c + 1) & 1
        held = {}

        def after0():
            if s1:
                held["lora"] = stage1_lora(c + 2)
            if s3:
                held["u"] = stage3_solve(slot3, tslot3, held["state"])

        def after1():
            if s3:
                stage3_out(c, slot3, held["state"], held["u"])

        def after2():
            if s1:
                held["front"] = stage1_decay(held["lora"])

        if s3:
            held["state"] = stage3_state(slot3)
        if s2:
            stage2(slot2, tslot2, {0: after0, 1: after1, 2: after2})
        else:
            after0(), after1(), after2()
        if s1:
            stage1_back(held["front"], slot1)

    n_steady = max(n_chunks - 2, 0)
    for c in range(-2, 0):
        step(c)
    if n_steady > 0:
        def body(c, carry):
            step(c)
            return carry

        lax.fori_loop(0, n_steady, body, 0)
    for c in range(n_steady, n_chunks):
        step(c)


def _rwkv(proj, params, *, batch, seq, lblk, n_pairs):
    (mu_r, mu_k, mu_v, mu_lo, w0, a0, k_k, k_a, r_k, ln_w, ln_b, ww2p, aw2p, gw2) = params
    t = batch * seq
    nsb = seq // lblk
    width = n_pairs * LANES
    ngrp = RWKV_DIM // width
    row_map = lambda c0: (lambda b, g, i: (b * nsb + i, c0 // width + g))
    grp = lambda b, g, i: (0, g)
    const = lambda b, g, i: (0, 0)
    vec = pl.BlockSpec((1, width), grp)
    rec = lambda slots, rows, cols, dtype: pltpu.VMEM((slots, n_pairs, rows, cols), dtype)
    return pl.pallas_call(
        functools.partial(_rwkv_kernel, n_chunks=lblk // CHUNK, n_pairs=n_pairs),
        grid=(batch, ngrp, nsb),
        in_specs=[
            pl.BlockSpec((lblk, width), row_map(COL_R)),
            pl.BlockSpec((lblk, width), row_map(COL_K)),
            pl.BlockSpec((lblk, width), row_map(COL_V)),
            pl.BlockSpec((lblk, LORA_WIDTH), lambda b, g, i: (b * nsb + i, COL_LORA // LORA_WIDTH)),
            vec, vec, vec,
            pl.BlockSpec((1, LORA_WIDTH), const),
            vec, vec, vec, vec, vec, vec, vec,
            pl.BlockSpec((DECAY_LORA, width), grp),
            pl.BlockSpec((AAA_LORA, width), grp),
            pl.BlockSpec((GATE_LORA, width), grp),
        ],
        out_specs=pl.BlockSpec((lblk, width), lambda b, g, i: (b * nsb + i, g)),
        out_shape=jax.ShapeDtypeStruct((t, RWKV_DIM), BF16),
        scratch_shapes=[
            pltpu.VMEM((1, width), F32), pltpu.VMEM((1, width), F32), pltpu.VMEM((1, width), F32),
            pltpu.VMEM((1, LORA_WIDTH), F32),
            pltpu.VMEM((n_pairs, LANES, LANES), F32),
            rec(3, CHUNK, LANES, F32),
            rec(3, CHUNK, 2 * LANES, BF16),
            rec(3, LANES, LANES, BF16),
            rec(3, LANES, LANES, BF16),
            rec(3, LANES, LANES, BF16),
            rec(3, CHUNK, LANES, F32),
            rec(3, CHUNK, LANES, BF16),
            rec(3, CHUNK, LANES, F32),
            rec(3, CHUNK, LANES, F32),
            rec(3, 1, LANES, F32),
            rec(2, CHUNK, LANES, BF16),
        ],
        compiler_params=pltpu.CompilerParams(
            dimension_semantics=("parallel", "parallel", "arbitrary"), vmem_limit_bytes=VMEM_LIMIT),
        name="rwkv",
    )(proj, proj, proj, proj, mu_r, mu_k, mu_v, mu_lo, w0, a0, k_k, k_a, r_k, ln_w, ln_b,
      ww2p, aw2p, gw2)


def _merge_kernel(x_ref, yg_ref, yr_ref, ga_ref, gb_ref, wg_ref, wr_ref, wo_ref, o_ref):
    mixed = (_sigmoid(ga_ref[...]) * jnp.dot(yg_ref[...], wg_ref[...], preferred_element_type=F32)
             + _sigmoid(gb_ref[...]) * jnp.dot(yr_ref[...], wr_ref[...], preferred_element_type=F32))
    o_ref[...] = x_ref[...] + _mm(mixed, wo_ref[...])


def _merge(x2, yg, yr, proj, wg, wr, wo, *, tm):
    t = x2.shape[0]
    tok = lambda i: (i, 0)
    const = lambda i: (0, 0)
    wspec = pl.BlockSpec((D_MODEL, D_MODEL), const)
    return pl.pallas_call(
        _merge_kernel,
        grid=(t // tm,),
        in_specs=[
            pl.BlockSpec((tm, D_MODEL), tok),
            pl.BlockSpec((tm, D_MODEL), tok),
            pl.BlockSpec((tm, D_MODEL), tok),
            pl.BlockSpec((tm, D_MODEL), lambda i: (i, COL_GATE // D_MODEL)),
            pl.BlockSpec((tm, D_MODEL), lambda i: (i, COL_GATE // D_MODEL + 1)),
            wspec, wspec, wspec,
        ],
        out_specs=pl.BlockSpec((tm, D_MODEL), tok),
        out_shape=jax.ShapeDtypeStruct((t, D_MODEL), F32),
        compiler_params=pltpu.CompilerParams(
            dimension_semantics=("parallel",), vmem_limit_bytes=VMEM_LIMIT),
        name="merge",
    )(x2, yg, yr, proj, proj, wg, wr, wo)


def _mlp_kernel(x_ref, g_ref, wu_ref, wd_ref, gf_ref, o_ref, *, final_norm):
    x = x_ref[...]
    h = (x * lax.rsqrt(jnp.mean(x * x, axis=-1, keepdims=True) + NORM_EPS) * g_ref[...]).astype(BF16)
    up = jnp.dot(h, wu_ref[...], preferred_element_type=F32)
    act = jnp.square(jnp.maximum(up, 0.0)).astype(BF16)
    x = x + jnp.dot(act, wd_ref[...], preferred_element_type=F32)
    if final_norm:
        x = x * lax.rsqrt(jnp.mean(x * x, axis=-1, keepdims=True) + NORM_EPS) * gf_ref[...]
    o_ref[...] = x


def _mlp(x2, g, wu, wd, gf, *, tm, final_norm):
    t = x2.shape[0]
    tok = lambda i: (i, 0)
    const = lambda i: (0, 0)
    return pl.pallas_call(
        functools.partial(_mlp_kernel, final_norm=final_norm),
        grid=(t // tm,),
        in_specs=[
            pl.BlockSpec((tm, D_MODEL), tok),
            pl.BlockSpec((1, D_MODEL), const),
            pl.BlockSpec((D_MODEL, D_FF), const),
            pl.BlockSpec((D_FF, D_MODEL), const),
            pl.BlockSpec((1, D_MODEL), const),
        ],
        out_specs=pl.BlockSpec((tm, D_MODEL), tok),
        out_shape=jax.ShapeDtypeStruct((t, D_MODEL), F32),
        compiler_params=pltpu.CompilerParams(
            dimension_semantics=("parallel",), vmem_limit_bytes=VMEM_LIMIT),
        name="mlp",
    )(x2, g, wu, wd, gf)


def _regroup_w_in(w):
    w = w.astype(BF16)
    gla_w = w[:, :GLA_WIDTH]
    rw = w[:, GLA_WIDTH:GLA_WIDTH + RWKV_WIDTH]
    gates = w[:, GLA_WIDTH + RWKV_WIDTH:]
    gla_main = GLA_WIDTH - GLA_GATE_RANK
    pad = jnp.zeros((w.shape[0], PROJ_WIDTH - COL_GAL - GLA_GATE_RANK), w.dtype)
    return jnp.concatenate([gates, gla_w[:, :gla_main], rw, gla_w[:, gla_main:], pad], axis=1)


def _row(v):
    return v.reshape(1, -1).astype(F32)


def kernel(x, norm_mix, w_in, gla_a_w2, gla_a_b, gla_norm, rwkv_mu, rwkv_w0, rwkv_w_w2, rwkv_a0, rwkv_a_w2, rwkv_g_w2, rwkv_k_k, rwkv_k_a, rwkv_r_k, rwkv_ln_w, rwkv_ln_b, w_branch_gla, w_branch_rwkv, w_out, norm_mlp, w_up, w_down, norm_final):
    batch, seq, d = x.shape
    assert d == D_MODEL and seq % CHUNK == 0
    t = batch * seq
    depth = norm_mix.shape[0]
    tm = 512 if t % 512 == 0 else CHUNK
    lblk = 1024 if seq % 1024 == 0 else CHUNK
    x2 = x.reshape(t, d)
    for l in range(depth):
        tn = PROJ_WIDTH // 2
        wp = _regroup_w_in(w_in[l]).reshape(D_MODEL, PROJ_WIDTH // tn, tn).transpose(1, 0, 2)
        proj = _inproj(x2, _row(norm_mix[l]), wp, tm=tm, tn=tn)

        w2p = jnp.zeros((LANES, GLA_KEY), BF16).at[:GLA_GATE_RANK].set(gla_a_w2[l].astype(BF16))
        y_gla = _gla(proj, w2p, _row(gla_a_b[l]), _row(gla_norm[l]), batch=batch, seq=seq, lblk=lblk)

        mu = rwkv_mu[l]
        params = (_row(mu[0:RWKV_DIM]), _row(mu[RWKV_DIM:2 * RWKV_DIM]), _row(mu[2 * RWKV_DIM:3 * RWKV_DIM]),
                  _row(mu[3 * RWKV_DIM:]), _row(rwkv_w0[l]), _row(rwkv_a0[l]), _row(rwkv_k_k[l]),
                  _row(rwkv_k_a[l]), _row(rwkv_r_k[l]), _row(rwkv_ln_w[l]), _row(rwkv_ln_b[l]),
                  rwkv_w_w2[l].astype(BF16), rwkv_a_w2[l].astype(BF16), rwkv_g_w2[l].astype(BF16))
        y_rwkv = _rwkv(proj, params, batch=batch, seq=seq, lblk=lblk, n_pairs=RWKV_DIM // LANES)

        x2 = _merge(x2, y_gla, y_rwkv, proj, w_branch_gla[l].astype(BF16), w_branch_rwkv[l].astype(BF16),
                    w_out[l].astype(BF16), tm=tm)
        x2 = _mlp(x2, _row(norm_mlp[l]), w_up[l].astype(BF16), w_down[l].astype(BF16), _row(norm_final),
                  tm=tm, final_norm=(l == depth - 1))
    return x2.reshape(batch, seq, d)
```

```python
import functools

import jax
import jax.numpy as jnp
from jax import lax
from jax.experimental import pallas as pl
from jax.experimental.pallas import tpu as pltpu

F32 = jnp.float32
BF16 = jnp.bfloat16

D_MODEL = 1024
GLA_HEADS = 4
GLA_KEY = D_MODEL // 2
GLA_VAL = D_MODEL
GLA_DK = GLA_KEY // GLA_HEADS
GLA_DV = GLA_VAL // GLA_HEADS
GLA_GATE_RANK = 16
GLA_GATE_NORM = 16.0
GLA_NORM_EPS = 1e-5
RWKV_HEAD = 64
RWKV_DIM = D_MODEL
DECAY_LORA = 64
AAA_LORA = 64
GATE_LORA = 128
RWKV_GN_EPS = 64e-5
KK_NORM_FLOOR = 1e-12
DECAY_SCALE = 0.6065306597126334
D_FF = 4 * D_MODEL
NORM_EPS = 1e-6

GLA_WIDTH = 2 * GLA_KEY + 2 * GLA_VAL + GLA_GATE_RANK
RWKV_WIDTH = 3 * RWKV_DIM + DECAY_LORA + AAA_LORA + GATE_LORA

LANES = 128
CHUNK = 64
COL_GATE = 0
COL_GLA_QK = 2048
COL_GLA_V = 3072
COL_GLA_OG = 4096
COL_R = 5120
COL_K = 6144
COL_V = 7168
COL_LORA = 8192
COL_GAL = 8448
PROJ_WIDTH = 8704
LORA_WIDTH = DECAY_LORA + AAA_LORA + GATE_LORA

VMEM_LIMIT = 56 * 1024 * 1024


def _mm(a, b):
    return jnp.dot(a.astype(BF16), b.astype(BF16), preferred_element_type=F32)


def _mm_nt(a, b):
    return lax.dot_general(a.astype(BF16), b.astype(BF16), (((1,), (1,)), ((), ())),
                           preferred_element_type=F32)


def _mm_tn(a, b):
    return lax.dot_general(a.astype(BF16), b.astype(BF16), (((0,), (0,)), ((), ())),
                           preferred_element_type=F32)


def _cumsum_rows(tri, x):
    tri = tri.astype(BF16)
    hi = x.astype(BF16)
    lo = (x - hi.astype(F32)).astype(BF16)
    return (jnp.dot(tri, hi, preferred_element_type=F32) + jnp.dot(tri, lo, preferred_element_type=F32))


def _softplus(x):
    return jnp.maximum(x, 0.0) + jnp.log(1.0 + jnp.exp(-jnp.abs(x)))


def _sigmoid(x):
    return 1.0 / (1.0 + jnp.exp(-x))


def _iota(shape, dim):
    return lax.broadcasted_iota(jnp.int32, shape, dim)


def _chunk_rows(c):
    if isinstance(c, int):
        return pl.ds(c * CHUNK, CHUNK)
    return pl.ds(pl.multiple_of(c * CHUNK, CHUNK), CHUNK)


def _inproj_kernel(x_ref, g_ref, w_ref, o_ref, h_ref):
    @pl.when(pl.program_id(1) == 0)
    def _():
        x = x_ref[...]
        ms = jnp.mean(x * x, axis=-1, keepdims=True)
        h_ref[...] = (x * lax.rsqrt(ms + NORM_EPS) * g_ref[...]).astype(BF16)

    o_ref[...] = jnp.dot(h_ref[...], w_ref[pl.program_id(1)], preferred_element_type=F32)


def _inproj(x2, g, wp, *, tm, tn):
    t = x2.shape[0]
    return pl.pallas_call(
        _inproj_kernel,
        grid=(t // tm, PROJ_WIDTH // tn),
        in_specs=[
            pl.BlockSpec((tm, D_MODEL), lambda i, j: (i, 0)),
            pl.BlockSpec((1, D_MODEL), lambda i, j: (0, 0)),
            pl.BlockSpec((PROJ_WIDTH // tn, D_MODEL, tn), lambda i, j: (0, 0, 0),
                         pipeline_mode=pl.Buffered(1)),
        ],
        out_specs=pl.BlockSpec((tm, tn), lambda i, j: (i, j)),
        out_shape=jax.ShapeDtypeStruct((t, PROJ_WIDTH), F32),
        scratch_shapes=[pltpu.VMEM((tm, D_MODEL), BF16)],
        compiler_params=pltpu.CompilerParams(
            dimension_semantics=("parallel", "arbitrary"), vmem_limit_bytes=VMEM_LIMIT),
        name="inproj",
    )(x2, g, wp)


def _gla_kernel(qk_ref, v_ref, og_ref, gal_ref, w2_ref, ab_ref, gn_ref, y_ref, st_ref,
                qi_ref, ki_ref, qd_ref, kd_ref, dec_ref, o_ref, *, n_chunks, wide):
    @pl.when(pl.program_id(1) == 0)
    def _():
        st_ref[...] = jnp.zeros_like(st_ref)

    row = _iota((CHUNK, CHUNK), 0)
    col = _iota((CHUNK, CHUNK), 1)
    causal = row >= col
    tri = causal.astype(BF16)
    w2 = w2_ref[...]
    ab = ab_ref[...]
    gn = gn_ref[...]
    heads = range(GLA_HEADS)
    ks = [slice(h * GLA_DK, (h + 1) * GLA_DK) for h in heads]
    vs = [slice(h * GLA_DV, (h + 1) * GLA_DV) for h in heads]

    group = range(wide)
    n_groups = n_chunks // wide

    def rows_of(g, j):
        return _chunk_rows(g * wide + j)

    def stage1_gate(g):
        return g, [_mm(gal_ref[rows_of(g, j), 0:GLA_GATE_RANK], w2) for j in group]

    def stage1_cumsum(gate):
        g, lins = gate
        las = [-_softplus(-(lin + ab)) * (1.0 / GLA_GATE_NORM) for lin in lins]
        return g, [_cumsum_rows(tri, la) for la in las]

    def stage1_store(cs, slot):
        g, bs = cs
        for j in group:
            b = bs[j]
            rows = rows_of(g, j)
            b_mid = b[CHUNK // 2 - 1:CHUNK // 2, :]
            b_last = b[CHUNK - 1:CHUNK, :]
            q = qk_ref[rows, 0:GLA_KEY] * (GLA_DK ** -0.5)
            k = qk_ref[rows, GLA_KEY:2 * GLA_KEY]
            e_in = jnp.exp(b - b_mid)
            qi = q * e_in
            ki = k * (1.0 / e_in)
            qi_ref[slot, j] = qi.astype(BF16)
            ki_ref[slot, j] = ki.astype(BF16)
            qd_ref[slot, j] = (qi * jnp.exp(b_mid)).astype(BF16)
            kd_ref[slot, j] = (ki * jnp.exp(b_last - b_mid)).astype(BF16)
            dec_ref[slot, j] = jnp.exp(b_last)

    def stage2_front(g, slot):
        v = [[v_ref[rows_of(g, j), vs[h]].astype(BF16) for h in heads] for j in group]
        st = [st_ref[h] for h in heads]
        s = [[_mm_nt(qi_ref[slot, j, :, ks[h]], ki_ref[slot, j, :, ks[h]]) for h in heads] for j in group]
        upd = [[_mm_tn(v[j][h], kd_ref[slot, j, :, ks[h]]) for h in heads] for j in group]
        inter0 = [_mm_nt(qd_ref[slot, 0, :, ks[h]], st[h]) for h in heads]
        return v, st, s, upd, inter0

    def stage2_back(g, slot, front):
        v, st, s, upd, inter0 = front
        intra = [[_mm(jnp.where(causal, s[j][h], 0.0), v[j][h]) for h in heads] for j in group]
        inter = [inter0]
        for j in group:
            st = [st[h] * dec_ref[slot, j, :, ks[h]] + upd[j][h] for h in heads]
            if j + 1 < wide:
                inter.append([_mm_nt(qd_ref[slot, j + 1, :, ks[h]], st[h]) for h in heads])
        for h in heads:
            st_ref[h] = st[h]
        for j in group:
            for h in heads:
                o_ref[slot, j, :, vs[h]] = intra[j][h] + inter[j][h]

    def stage3(g, slot):
        for j in group:
            rows = rows_of(g, j)
            for h in heads:
                o = o_ref[slot, j, :, vs[h]]
                o = o * lax.rsqrt(jnp.mean(o * o, axis=-1, keepdims=True) + GLA_NORM_EPS) * gn
                og = og_ref[rows, vs[h]]
                y_ref[rows, vs[h]] = (o * (og * _sigmoid(og))).astype(BF16)

    def step(g):
        s1, s2, s3 = [0 <= g + k < n_groups if isinstance(g, int) else True for k in (1, 0, -1)]
        slot = g & 1
        if s3:
            stage3(g - 1, 1 - slot)
        if s1:
            gate = stage1_gate(g + 1)
        if s2:
            front = stage2_front(g, slot)
        if s1:
            sums = stage1_cumsum(gate)
        if s2:
            stage2_back(g, slot, front)
        if s1:
            stage1_store(sums, 1 - slot)

    step(-1)
    step(0)
    if n_groups > 2:
        def body(g, carry):
            step(g)
            return carry

        lax.fori_loop(1, n_groups - 1, body, 0)
    if n_groups > 1:
        step(n_groups - 1)
    step(n_groups)


def _gla(proj, w2p, ab, gn, *, batch, seq, lblk):
    t = batch * seq
    nsb = seq // lblk
    row_map = lambda cb: (lambda b, i: (b * nsb + i, cb))
    const = lambda b, i: (0, 0)
    n_chunks = lblk // CHUNK
    wide = next(w for w in (4, 2, 1) if n_chunks % w == 0)
    hand = lambda rows, dtype: pltpu.VMEM((2, wide, rows, GLA_KEY), dtype)
    return pl.pallas_call(
        functools.partial(_gla_kernel, n_chunks=n_chunks, wide=wide),
        grid=(batch, nsb),
        in_specs=[
            pl.BlockSpec((lblk, 2 * GLA_KEY), row_map(COL_GLA_QK // (2 * GLA_KEY))),
            pl.BlockSpec((lblk, GLA_VAL), row_map(COL_GLA_V // GLA_VAL)),
            pl.BlockSpec((lblk, GLA_VAL), row_map(COL_GLA_OG // GLA_VAL)),
            pl.BlockSpec((lblk, LANES), row_map(COL_GAL // LANES)),
            pl.BlockSpec((GLA_GATE_RANK, GLA_KEY), const),
            pl.BlockSpec((1, GLA_KEY), const),
            pl.BlockSpec((1, GLA_DV), const),
        ],
        out_specs=pl.BlockSpec((lblk, GLA_VAL), lambda b, i: (b * nsb + i, 0)),
        out_shape=jax.ShapeDtypeStruct((t, GLA_VAL), BF16),
        scratch_shapes=[
            pltpu.VMEM((GLA_HEADS, GLA_DV, GLA_DK), F32),
            hand(CHUNK, BF16), hand(CHUNK, BF16), hand(CHUNK, BF16), hand(CHUNK, BF16),
            hand(1, F32),
            pltpu.VMEM((2, wide, CHUNK, GLA_VAL), F32),
        ],
        compiler_params=pltpu.CompilerParams(
            dimension_semantics=("parallel", "arbitrary"), vmem_limit_bytes=VMEM_LIMIT),
        name="gla",
    )(proj, proj, proj, proj, w2p, ab, gn)


def _pair_block_diag(y, lo):
    zero = jnp.zeros_like(y)
    return jnp.concatenate([jnp.where(lo, y, zero), jnp.where(lo, zero, y)], axis=0)


def _unit_lower_inverse(ns, masks, after_stage=None):
    after_stage = after_stage or {}
    stage = [0]

    def stage_done():
        hook = after_stage.get(stage[0])
        if hook is not None:
            hook()
        stage[0] += 1

    eye, blk16, off32, off64, lo = masks
    nd = [jnp.where(blk16, n, 0.0) for n in ns]
    x = [eye + d for d in nd]
    nd = [d.astype(BF16) for d in nd]
    m = [_mm(d, _pair_block_diag(d, lo)).astype(BF16) for d in nd]
    stage_done()
    c = ns[0].shape[0]
    for step in range(3):
        mb = [_pair_block_diag(mi, lo) for mi in m]
        if step < 2:
            xm = [_mm(jnp.concatenate([xi.astype(BF16), mi], axis=0), mbi) for xi, mi, mbi in zip(x, m, mb)]
            m = [r[c:].astype(BF16) for r in xm]
            xm = [r[:c] for r in xm]
        else:
            xm = [_mm(xi, mbi) for xi, mbi in zip(x, mb)]
        x = [xi + xmi for xi, xmi in zip(x, xm)]
        stage_done()
    for off in (off32, off64):
        xb = [xi.astype(BF16) for xi in x]
        xn = [_mm(xi, _pair_block_diag(jnp.where(off, n, 0.0).astype(BF16), lo)) for xi, n in zip(xb, ns)]
        stage_done()
        xnx = [_mm(a, _pair_block_diag(xi, lo)) for a, xi in zip(xn, xb)]
        x = [xi + b for xi, b in zip(x, xnx)]
        stage_done()
    return x


def _rwkv_kernel(r_ref, k_ref, v_ref, lo_ref, mu_r_ref, mu_k_ref, mu_v_ref, mu_lo_ref,
                 w0_ref, a0_ref, kk_ref, ka_ref, rk_ref, lnw_ref, lnb_ref,
                 ww2_ref, aw2_ref, gw2_ref, y_ref,
                 pr_ref, pk_ref, pv_ref, plo_ref, st_ref,
                 n_ref, prm_ref, ahl_ref, bk2_ref, rkv_ref, akv_ref, v2_ref, bonv_ref, gate_ref, elast_ref,
                 t_ref,
                 *, n_chunks, n_pairs):
    @pl.when(pl.program_id(2) == 0)
    def _():
        pr_ref[...] = jnp.zeros_like(pr_ref)
        pk_ref[...] = jnp.zeros_like(pk_ref)
        pv_ref[...] = jnp.zeros_like(pv_ref)
        plo_ref[...] = jnp.zeros_like(plo_ref)
        st_ref[...] = jnp.zeros_like(st_ref)

    width = n_pairs * LANES
    tri = (_iota((CHUNK, CHUNK), 0) >= _iota((CHUNK, CHUNK), 1)).astype(BF16)

    trow = _iota((CHUNK, LANES), 0)
    tcol = _iota((CHUNK, LANES), 1) & (RWKV_HEAD - 1)
    lo = _iota((CHUNK, LANES), 1) < RWKV_HEAD
    strict = trow > tcol
    incl = trow >= tcol
    blk16 = (trow // 16) == (tcol // 16)
    blk32 = (trow // 32) == (tcol // 32)
    inv_masks = ((trow == tcol).astype(F32), blk16, blk32 & (~blk16), ~blk32, lo)
    srow = _iota((LANES, LANES), 0)
    scol = _iota((LANES, LANES), 1)
    same = (srow < RWKV_HEAD) == (scol < RWKV_HEAD)
    first_w = _iota((CHUNK, width), 0) == 0
    first_lo = _iota((CHUNK, LORA_WIDTH), 0) == 0
    pairs = range(n_pairs)
    ls = [slice(p * LANES, (p + 1) * LANES) for p in pairs]

    def seg_sum(x):
        s0 = jnp.sum(jnp.where(lo, x, 0.0), axis=-1, keepdims=True)
        s1 = jnp.sum(jnp.where(lo, 0.0, x), axis=-1, keepdims=True)
        return jnp.where(lo, s0, s1)

    def stack(a, b):
        return jnp.concatenate([a, b], axis=0)

    def shifted(z, prev_ref, first):
        zs = jnp.where(first, prev_ref[...], pltpu.roll(z, 1, axis=0))
        prev_ref[...] = z[CHUNK - 1:CHUNK, :]
        return zs

    def stage1_lora(c):
        rows = _chunk_rows(c)
        zl = lo_ref[rows, :]
        lora = zl + (shifted(zl, plo_ref, first_lo) - zl) * mu_lo_ref[...]
        wl = lora[:, 0:DECAY_LORA]
        al = lora[:, DECAY_LORA:DECAY_LORA + AAA_LORA]
        gl = lora[:, DECAY_LORA + AAA_LORA:LORA_WIDTH]
        return (c, _mm(jnp.tanh(wl), ww2_ref[...]), _mm(al, aw2_ref[...]),
                _mm(_sigmoid(gl), gw2_ref[...]))

    def stage1_decay(lora_out):
        c, w_lin, a_lin, gate_all = lora_out
        logw_all = (-DECAY_SCALE) * _sigmoid(w0_ref[...] + w_lin)
        return c, logw_all, a_lin, gate_all, _cumsum_rows(tri, logw_all)

    def stage1_back(front, slot):
        c, logw_all, a_lin, gate_all, lw_all = front
        rows = _chunk_rows(c)
        zr = r_ref[rows, :]
        zk = k_ref[rows, :]
        zv = v_ref[rows, :]
        r_all = zr + (shifted(zr, pr_ref, first_w) - zr) * mu_r_ref[...]
        k_all = zk + (shifted(zk, pk_ref, first_w) - zk) * mu_k_ref[...]
        v_all = zv + (shifted(zv, pv_ref, first_w) - zv) * mu_v_ref[...]
        a_all = _sigmoid(a0_ref[...] + a_lin)

        r = [r_all[:, s] for s in ls]
        k = [k_all[:, s] for s in ls]
        v = [v_all[:, s] for s in ls]
        a = [a_all[:, s] for s in ls]
        lw = [lw_all[:, s] for s in ls]
        kk = [k[p] * kk_ref[:, ls[p]] for p in pairs]
        kkn = [kk[p] * lax.rsqrt(jnp.maximum(seg_sum(kk[p] * kk[p]), KK_NORM_FLOOR ** 2)) for p in pairs]
        kmod = [k[p] * (1.0 + (a[p] - 1.0) * ka_ref[:, ls[p]]) for p in pairs]
        bvec = [kkn[p] * a[p] for p in pairs]
        mid = [x[CHUNK // 2 - 1:CHUNK // 2, :] for x in lw]
        last = [x[CHUNK - 1:CHUNK, :] for x in lw]
        zero = jnp.zeros((CHUNK, LANES), BF16)
        e_in = [jnp.exp(lw[p] - mid[p]) for p in pairs]
        e_ng = [1.0 / e_in[p] for p in pairs]
        ag_f = [-kkn[p] * (e_in[p] * jnp.exp(-logw_all[:, ls[p]])) for p in pairs]
        rg_f = [r[p] * e_in[p] for p in pairs]
        bg_f = [bvec[p] * e_ng[p] for p in pairs]
        kg_f = [kmod[p] * e_ng[p] for p in pairs]
        ag = [x.astype(BF16) for x in ag_f]
        rg = [x.astype(BF16) for x in rg_f]
        v16 = [v[p].astype(BF16) for p in pairs]
        pm = [_mm_nt(jnp.concatenate([jnp.where(lo, ag[p], zero), jnp.where(lo, rg[p], zero),
                                      jnp.where(lo, zero, ag[p]), jnp.where(lo, zero, rg[p])], axis=0),
                     stack(bg_f[p].astype(BF16), kg_f[p].astype(BF16))) for p in pairs]
        pm0 = [x[0:2 * CHUNK] for x in pm]
        pm1 = [pltpu.roll(x[2 * CHUNK:4 * CHUNK], RWKV_HEAD, axis=1) for x in pm]
        ak = [jnp.where(strict, jnp.where(lo, pm1[p][0:CHUNK], pm0[p][0:CHUNK]), 0.0) for p in pairs]
        rk = [jnp.where(incl, jnp.where(lo, pm1[p][CHUNK:], pm0[p][CHUNK:]), 0.0) for p in pairs]
        vx = [stack(jnp.where(lo, zero, v16[p]), jnp.where(lo, v16[p], zero)) for p in pairs]
        akrk = [_mm(stack(ak[p], rk[p]), vx[p]) for p in pairs]
        for p in pairs:
            e_mid = jnp.exp(mid[p])
            e_tail = jnp.exp(last[p] - mid[p])
            n_ref[slot, p] = jnp.where(strict, jnp.where(lo, pm0[p][0:CHUNK], pm1[p][0:CHUNK]), 0.0)
            rb = jnp.where(incl, jnp.where(lo, pm0[p][CHUNK:], pm1[p][CHUNK:]), 0.0)
            prm_ref[slot, p] = rb.astype(BF16)
            akv_ref[slot, p] = akrk[p][0:CHUNK]
            rkv_ref[slot, p] = akrk[p][CHUNK:]
            ahl_ref[slot, p] = stack((ag_f[p] * e_mid).astype(BF16), (rg_f[p] * e_mid).astype(BF16))
            bk2_ref[slot, p] = stack((bg_f[p] * e_tail).astype(BF16), (kg_f[p] * e_tail).astype(BF16))
            v2_ref[slot, p] = v16[p]
            bonv_ref[slot, p] = seg_sum(r[p] * kmod[p] * rk_ref[:, ls[p]]) * v[p]
            gate_ref[slot, p] = gate_all[:, ls[p]]
            elast_ref[slot, p] = jnp.exp(last[p])

    def stage2(slot, tslot, after_stage=None):
        tinv = _unit_lower_inverse([n_ref[slot, p] for p in pairs], inv_masks, after_stage)
        for p in pairs:
            t_ref[tslot, p] = tinv[p].astype(BF16)

    def stage3_state(slot):
        st = [st_ref[p] for p in pairs]
        ah = [_mm_nt(ahl_ref[slot, p], st[p]) for p in pairs]
        return st, ah

    def stage3_solve(slot, tslot, front):
        st, ah = front
        w = [(ah[p][0:CHUNK] + akv_ref[slot, p]).astype(BF16) for p in pairs]
        return [_mm(t_ref[tslot, p], _pair_block_diag(w[p], lo)).astype(BF16) for p in pairs]

    def stage3_out(c, slot, front, u):
        st, ah = front
        rows = _chunk_rows(c)
        ys = [_mm(prm_ref[slot, p], _pair_block_diag(u[p], lo)) + rkv_ref[slot, p] for p in pairs]
        dh = [_mm_tn(stack(u[p], v2_ref[slot, p]), bk2_ref[slot, p]) for p in pairs]
        for p in pairs:
            st_ref[p] = st[p] * elast_ref[slot, p] + jnp.where(same, dh[p], 0.0)
            y = ah[p][CHUNK:] + ys[p]
            mean = seg_sum(y) * (1.0 / RWKV_HEAD)
            d = y - mean
            var = seg_sum(d * d) * (1.0 / RWKV_HEAD)
            yn = d * lax.rsqrt(var + RWKV_GN_EPS) * lnw_ref[:, ls[p]] + lnb_ref[:, ls[p]]
            y_ref[rows, ls[p]] = ((yn + bonv_ref[slot, p]) * gate_ref[slot, p]).astype(BF16)

    def step(c):
        s1, s2, s3 = [0 <= c + k < n_chunks if isinstance(c, int) else True for k in (2, 1, 0)]
        slot3, slot2, slot1 = c % 3, (c + 1) % 3, (c + 2) % 3
        tslot3, tslot2 = c & 1, (c + 1) & 1
        held = {}

        def after0():
            if s1:
                held["lora"] = stage1_lora(c + 2)
            if s3:
                held["u"] = stage3_solve(slot3, tslot3, held["state"])

        def after1():
            if s3:
                stage3_out(c, slot3, held["state"], held["u"])

        def after2():
            if s1:
                held["front"] = stage1_decay(held["lora"])

        if s3:
            held["state"] = stage3_state(slot3)
        if s2:
            stage2(slot2, tslot2, {0: after0, 1: after1, 2: after2})
        else:
            after0(), after1(), after2()
        if s1:
            stage1_back(held["front"], slot1)

    n_steady = max(n_chunks - 2, 0)
    for c in range(-2, 0):
        step(c)
    if n_steady > 0:
        def body(c, carry):
            step(c)
            return carry

        lax.fori_loop(0, n_steady, body, 0)
    for c in range(n_steady, n_chunks):
        step(c)


def _rwkv(proj, params, *, batch, seq, lblk, n_pairs):
    (mu_r, mu_k, mu_v, mu_lo, w0, a0, k_k, k_a, r_k, ln_w, ln_b, ww2p, aw2p, gw2) = params
    t = batch * seq
    nsb = seq // lblk
    width = n_pairs * LANES
    ngrp = RWKV_DIM // width
    row_map = lambda c0: (lambda b, g, i: (b * nsb + i, c0 // width + g))
    grp = lambda b, g, i: (0, g)
    const = lambda b, g, i: (0, 0)
    vec = pl.BlockSpec((1, width), grp)
    rec = lambda slots, rows, cols, dtype: pltpu.VMEM((slots, n_pairs, rows, cols), dtype)
    return pl.pallas_call(
        functools.partial(_rwkv_kernel, n_chunks=lblk // CHUNK, n_pairs=n_pairs),
        grid=(batch, ngrp, nsb),
        in_specs=[
            pl.BlockSpec((lblk, width), row_map(COL_R)),
            pl.BlockSpec((lblk, width), row_map(COL_K)),
            pl.BlockSpec((lblk, width), row_map(COL_V)),
            pl.BlockSpec((lblk, LORA_WIDTH), lambda b, g, i: (b * nsb + i, COL_LORA // LORA_WIDTH)),
            vec, vec, vec,
            pl.BlockSpec((1, LORA_WIDTH), const),
            vec, vec, vec, vec, vec, vec, vec,
            pl.BlockSpec((DECAY_LORA, width), grp),
            pl.BlockSpec((AAA_LORA, width), grp),
            pl.BlockSpec((GATE_LORA, width), grp),
        ],
        out_specs=pl.BlockSpec((lblk, width), lambda b, g, i: (b * nsb + i, g)),
        out_shape=jax.ShapeDtypeStruct((t, RWKV_DIM), BF16),
        scratch_shapes=[
            pltpu.VMEM((1, width), F32), pltpu.VMEM((1, width), F32), pltpu.VMEM((1, width), F32),
            pltpu.VMEM((1, LORA_WIDTH), F32),
            pltpu.VMEM((n_pairs, LANES, LANES), F32),
            rec(3, CHUNK, LANES, F32),
            rec(3, CHUNK, LANES, BF16),
            rec(3, LANES, LANES, BF16),
            rec(3, LANES, LANES, BF16),
            rec(3, CHUNK, LANES, F32),
            rec(3, CHUNK, LANES, F32),
            rec(3, CHUNK, LANES, BF16),
            rec(3, CHUNK, LANES, F32),
            rec(3, CHUNK, LANES, F32),
            rec(3, 1, LANES, F32),
            rec(2, CHUNK, LANES, BF16),
        ],
        compiler_params=pltpu.CompilerParams(
            dimension_semantics=("parallel", "parallel", "arbitrary"), vmem_limit_bytes=VMEM_LIMIT),
        name="rwkv",
    )(proj, proj, proj, proj, mu_r, mu_k, mu_v, mu_lo, w0, a0, k_k, k_a, r_k, ln_w, ln_b,
      ww2p, aw2p, gw2)


def _merge_kernel(x_ref, yg_ref, yr_ref, ga_ref, gb_ref, wg_ref, wr_ref, wo_ref, o_ref):
    mixed = (_sigmoid(ga_ref[...]) * jnp.dot(yg_ref[...], wg_ref[...], preferred_element_type=F32)
             + _sigmoid(gb_ref[...]) * jnp.dot(yr_ref[...], wr_ref[...], preferred_element_type=F32))
    o_ref[...] = x_ref[...] + _mm(mixed, wo_ref[...])


def _merge(x2, yg, yr, proj, wg, wr, wo, *, tm):
    t = x2.shape[0]
    tok = lambda i: (i, 0)
    const = lambda i: (0, 0)
    wspec = pl.BlockSpec((D_MODEL, D_MODEL), const)
    return pl.pallas_call(
        _merge_kernel,
        grid=(t // tm,),
        in_specs=[
            pl.BlockSpec((tm, D_MODEL), tok),
            pl.BlockSpec((tm, D_MODEL), tok),
            pl.BlockSpec((tm, D_MODEL), tok),
            pl.BlockSpec((tm, D_MODEL), lambda i: (i, COL_GATE // D_MODEL)),
            pl.BlockSpec((tm, D_MODEL), lambda i: (i, COL_GATE // D_MODEL + 1)),
            wspec, wspec, wspec,
        ],
        out_specs=pl.BlockSpec((tm, D_MODEL), tok),
        out_shape=jax.ShapeDtypeStruct((t, D_MODEL), F32),
        compiler_params=pltpu.CompilerParams(
            dimension_semantics=("parallel",), vmem_limit_bytes=VMEM_LIMIT),
        name="merge",
    )(x2, yg, yr, proj, proj, wg, wr, wo)


def _mlp_kernel(x_ref, g_ref, wu_ref, wd_ref, gf_ref, o_ref, *, final_norm):
    x = x_ref[...]
    h = (x * lax.rsqrt(jnp.mean(x * x, axis=-1, keepdims=True) + NORM_EPS) * g_ref[...]).astype(BF16)
    up = jnp.dot(h, wu_ref[...], preferred_element_type=F32)
    act = jnp.square(jnp.maximum(up, 0.0)).astype(BF16)
    x = x + jnp.dot(act, wd_ref[...], preferred_element_type=F32)
    if final_norm:
        x = x * lax.rsqrt(jnp.mean(x * x, axis=-1, keepdims=True) + NORM_EPS) * gf_ref[...]
    o_ref[...] = x


def _mlp(x2, g, wu, wd, gf, *, tm, final_norm):
    t = x2.shape[0]
    tok = lambda i: (i, 0)
    const = lambda i: (0, 0)
    return pl.pallas_call(
        functools.partial(_mlp_kernel, final_norm=final_norm),
        grid=(t // tm,),
        in_specs=[
            pl.BlockSpec((tm, D_MODEL), tok),
            pl.BlockSpec((1, D_MODEL), const),
            pl.BlockSpec((D_MODEL, D_FF), const),
            pl.BlockSpec((D_FF, D_MODEL), const),
            pl.BlockSpec((1, D_MODEL), const),
        ],
        out_specs=pl.BlockSpec((tm, D_MODEL), tok),
        out_shape=jax.ShapeDtypeStruct((t, D_MODEL), F32),
        compiler_params=pltpu.CompilerParams(
            dimension_semantics=("parallel",), vmem_limit_bytes=VMEM_LIMIT),
        name="mlp",
    )(x2, g, wu, wd, gf)


def _regroup_w_in(w):
    w = w.astype(BF16)
    gla_w = w[:, :GLA_WIDTH]
    rw = w[:, GLA_WIDTH:GLA_WIDTH + RWKV_WIDTH]
    gates = w[:, GLA_WIDTH + RWKV_WIDTH:]
    gla_main = GLA_WIDTH - GLA_GATE_RANK
    pad = jnp.zeros((w.shape[0], PROJ_WIDTH - COL_GAL - GLA_GATE_RANK), w.dtype)
    return jnp.concatenate([gates, gla_w[:, :gla_main], rw, gla_w[:, gla_main:], pad], axis=1)


def _row(v):
    return v.reshape(1, -1).astype(F32)


def kernel(x, norm_mix, w_in, gla_a_w2, gla_a_b, gla_norm, rwkv_mu, rwkv_w0, rwkv_w_w2, rwkv_a0, rwkv_a_w2, rwkv_g_w2, rwkv_k_k, rwkv_k_a, rwkv_r_k, rwkv_ln_w, rwkv_ln_b, w_branch_gla, w_branch_rwkv, w_out, norm_mlp, w_up, w_down, norm_final):
    batch, seq, d = x.shape
    assert d == D_MODEL and seq % CHUNK == 0
    t = batch * seq
    depth = norm_mix.shape[0]
    tm = 512 if t % 512 == 0 else CHUNK
    lblk = 1024 if seq % 1024 == 0 else CHUNK
    x2 = x.reshape(t, d)
    for l in range(depth):
        tn = PROJ_WIDTH // 2
        wp = _regroup_w_in(w_in[l]).reshape(D_MODEL, PROJ_WIDTH // tn, tn).transpose(1, 0, 2)
        proj = _inproj(x2, _row(norm_mix[l]), wp, tm=tm, tn=tn)

        y_gla = _gla(proj, gla_a_w2[l].astype(BF16), _row(gla_a_b[l]), _row(gla_norm[l]),
                     batch=batch, seq=seq, lblk=lblk)

        mu = rwkv_mu[l]
        params = (_row(mu[0:RWKV_DIM]), _row(mu[RWKV_DIM:2 * RWKV_DIM]), _row(mu[2 * RWKV_DIM:3 * RWKV_DIM]),
                  _row(mu[3 * RWKV_DIM:]), _row(rwkv_w0[l]), _row(rwkv_a0[l]), _row(rwkv_k_k[l]),
                  _row(rwkv_k_a[l]), _row(rwkv_r_k[l]), _row(rwkv_ln_w[l]), _row(rwkv_ln_b[l]),
                  rwkv_w_w2[l].astype(BF16), rwkv_a_w2[l].astype(BF16), rwkv_g_w2[l].astype(BF16))
        y_rwkv = _rwkv(proj, params, batch=batch, seq=seq, lblk=lblk, n_pairs=RWKV_DIM // LANES)

        x2 = _merge(x2, y_gla, y_rwkv, proj, w_branch_gla[l].astype(BF16), w_branch_rwkv[l].astype(BF16),
                    w_out[l].astype(BF16), tm=tm)
        x2 = _mlp(x2, _row(norm_mlp[l]), w_up[l].astype(BF16), w_down[l].astype(BF16), _row(norm_final),
                  tm=tm, final_norm=(l == depth - 1))
    return x2.reshape(batch, seq, d)
```

```python
import functools

import jax
import jax.numpy as jnp
from jax import lax
from jax.experimental import pallas as pl
from jax.experimental.pallas import tpu as pltpu

F32 = jnp.float32
BF16 = jnp.bfloat16

D_MODEL = 1024
GLA_HEADS = 4
GLA_KEY = D_MODEL // 2
GLA_VAL = D_MODEL
GLA_DK = GLA_KEY // GLA_HEADS
GLA_DV = GLA_VAL // GLA_HEADS
GLA_GATE_RANK = 16
GLA_GATE_NORM = 16.0
GLA_NORM_EPS = 1e-5
RWKV_HEAD = 64
RWKV_DIM = D_MODEL
DECAY_LORA = 64
AAA_LORA = 64
GATE_LORA = 128
RWKV_GN_EPS = 64e-5
KK_NORM_FLOOR = 1e-12
DECAY_SCALE = 0.6065306597126334
D_FF = 4 * D_MODEL
NORM_EPS = 1e-6

GLA_WIDTH = 2 * GLA_KEY + 2 * GLA_VAL + GLA_GATE_RANK
RWKV_WIDTH = 3 * RWKV_DIM + DECAY_LORA + AAA_LORA + GATE_LORA

LANES = 128
MXU_COLS = 256
CHUNK = 64
TOKEN_TILE = 512
SEQ_BLOCK = 1024
LORA_WIDTH = DECAY_LORA + AAA_LORA + GATE_LORA
COL_GATE = 0
COL_GLA_QK = COL_GATE + 2 * D_MODEL
COL_GLA_V = COL_GLA_QK + 2 * GLA_KEY
COL_GLA_OG = COL_GLA_V + GLA_VAL
COL_R = COL_GLA_OG + GLA_VAL
COL_K = COL_R + RWKV_DIM
COL_V = COL_K + RWKV_DIM
COL_LORA = COL_V + RWKV_DIM
COL_GAL = COL_LORA + LORA_WIDTH
PROJ_WIDTH = -(-(COL_GAL + GLA_GATE_RANK) // (2 * MXU_COLS)) * (2 * MXU_COLS)

VMEM_CAPACITY = 64 * 1024 * 1024
VMEM_LIMIT = VMEM_CAPACITY * 7 // 8


def _mm(a, b):
    return jnp.dot(a.astype(BF16), b.astype(BF16), preferred_element_type=F32)


def _mm_nt(a, b):
    return lax.dot_general(a.astype(BF16), b.astype(BF16), (((1,), (1,)), ((), ())),
                           preferred_element_type=F32)


def _mm_tn(a, b):
    return lax.dot_general(a.astype(BF16), b.astype(BF16), (((0,), (0,)), ((), ())),
                           preferred_element_type=F32)


def _cumsum_rows(tri, x):
    tri = tri.astype(BF16)
    hi = x.astype(BF16)
    lo = (x - hi.astype(F32)).astype(BF16)
    return jnp.dot(jnp.concatenate([tri, tri], axis=1), jnp.concatenate([hi, lo], axis=0),
                   preferred_element_type=F32)


def _softplus(x):
    return jnp.maximum(x, 0.0) + jnp.log(1.0 + jnp.exp(-jnp.abs(x)))


def _sigmoid(x):
    return 1.0 / (1.0 + jnp.exp(-x))


def _iota(shape, dim):
    return lax.broadcasted_iota(jnp.int32, shape, dim)


def _chunk_rows(c):
    if isinstance(c, int):
        return pl.ds(c * CHUNK, CHUNK)
    return pl.ds(pl.multiple_of(c * CHUNK, CHUNK), CHUNK)


def _inproj_kernel(x_ref, g_ref, w_ref, o_ref, h_ref):
    @pl.when(pl.program_id(1) == 0)
    def _():
        x = x_ref[...]
        ms = jnp.mean(x * x, axis=-1, keepdims=True)
        h_ref[...] = (x * lax.rsqrt(ms + NORM_EPS) * g_ref[...]).astype(BF16)

    o_ref[...] = jnp.dot(h_ref[...], w_ref[pl.program_id(1)], preferred_element_type=F32)


def _inproj(x2, g, wp, *, tm, tn):
    t = x2.shape[0]
    return pl.pallas_call(
        _inproj_kernel,
        grid=(t // tm, PROJ_WIDTH // tn),
        in_specs=[
            pl.BlockSpec((tm, D_MODEL), lambda i, j: (i, 0)),
            pl.BlockSpec((1, D_MODEL), lambda i, j: (0, 0)),
            pl.BlockSpec((PROJ_WIDTH // tn, D_MODEL, tn), lambda i, j: (0, 0, 0),
                         pipeline_mode=pl.Buffered(1)),
        ],
        out_specs=pl.BlockSpec((tm, tn), lambda i, j: (i, j)),
        out_shape=jax.ShapeDtypeStruct((t, PROJ_WIDTH), F32),
        scratch_shapes=[pltpu.VMEM((tm, D_MODEL), BF16)],
        compiler_params=pltpu.CompilerParams(
            dimension_semantics=("parallel", "arbitrary"), vmem_limit_bytes=VMEM_LIMIT),
        name="inproj",
    )(x2, g, wp)


def _gla_kernel(qk_ref, v_ref, og_ref, gal_ref, w2_ref, ab_ref, gn_ref, y_ref, st_ref,
                qi_ref, ki_ref, qd_ref, kd_ref, dec_ref, o_ref, *, n_chunks, wide):
    @pl.when(pl.program_id(1) == 0)
    def _():
        st_ref[...] = jnp.zeros_like(st_ref)

    row = _iota((CHUNK, CHUNK), 0)
    col = _iota((CHUNK, CHUNK), 1)
    causal = row >= col
    tri = causal.astype(BF16)
    w2 = w2_ref[...]
    ab = ab_ref[...]
    gn = gn_ref[...]
    heads = range(GLA_HEADS)
    ks = [slice(h * GLA_DK, (h + 1) * GLA_DK) for h in heads]
    vs = [slice(h * GLA_DV, (h + 1) * GLA_DV) for h in heads]

    group = range(wide)
    n_groups = n_chunks // wide

    def rows_of(g, j):
        return _chunk_rows(g * wide + j)

    def stage1_gate(g):
        return g, [_mm(gal_ref[rows_of(g, j), 0:GLA_GATE_RANK], w2) for j in group]

    def stage1_cumsum(gate):
        g, lins = gate
        las = [-_softplus(-(lin + ab)) * (1.0 / GLA_GATE_NORM) for lin in lins]
        return g, [_cumsum_rows(tri, la) for la in las]

    def stage1_store(cs, slot):
        g, bs = cs
        for j in group:
            b = bs[j]
            rows = rows_of(g, j)
            b_mid = b[CHUNK // 2 - 1:CHUNK // 2, :]
            b_last = b[CHUNK - 1:CHUNK, :]
            q = qk_ref[rows, 0:GLA_KEY] * (GLA_DK ** -0.5)
            k = qk_ref[rows, GLA_KEY:2 * GLA_KEY]
            e_in = jnp.exp(b - b_mid)
            qi = q * e_in
            ki = k * (1.0 / e_in)
            qi_ref[slot, j] = qi.astype(BF16)
            ki_ref[slot, j] = ki.astype(BF16)
            qd_ref[slot, j] = (qi * jnp.exp(b_mid)).astype(BF16)
            kd_ref[slot, j] = (ki * jnp.exp(b_last - b_mid)).astype(BF16)
            dec_ref[slot, j] = jnp.exp(b_last)

    def stage2_front(g, slot):
        v = [[v_ref[rows_of(g, j), vs[h]].astype(BF16) for h in heads] for j in group]
        st = [st_ref[h] for h in heads]
        s = [[_mm_nt(qi_ref[slot, j, :, ks[h]], ki_ref[slot, j, :, ks[h]]) for h in heads] for j in group]
        upd = [[_mm_tn(v[j][h], kd_ref[slot, j, :, ks[h]]) for h in heads] for j in group]
        inter0 = [_mm_nt(qd_ref[slot, 0, :, ks[h]], st[h]) for h in heads]
        return v, st, s, upd, inter0

    def stage2_back(g, slot, front):
        v, st, s, upd, inter0 = front
        intra = [[_mm(jnp.where(causal, s[j][h], 0.0), v[j][h]) for h in heads] for j in group]
        inter = [inter0]
        for j in group:
            st = [st[h] * dec_ref[slot, j, :, ks[h]] + upd[j][h] for h in heads]
            if j + 1 < wide:
                inter.append([_mm_nt(qd_ref[slot, j + 1, :, ks[h]], st[h]) for h in heads])
        for h in heads:
            st_ref[h] = st[h]
        for j in group:
            for h in heads:
                o_ref[slot, j, :, vs[h]] = intra[j][h] + inter[j][h]

    def stage3(g, slot):
        for j in group:
            rows = rows_of(g, j)
            for h in heads:
                o = o_ref[slot, j, :, vs[h]]
                o = o * lax.rsqrt(jnp.mean(o * o, axis=-1, keepdims=True) + GLA_NORM_EPS) * gn
                og = og_ref[rows, vs[h]]
                y_ref[rows, vs[h]] = (o * (og * _sigmoid(og))).astype(BF16)

    def step(g):
        s1, s2, s3 = [0 <= g + k < n_groups if isinstance(g, int) else True for k in (1, 0, -1)]
        slot = g & 1
        if s3:
            stage3(g - 1, 1 - slot)
        if s1:
            gate = stage1_gate(g + 1)
        if s2:
            front = stage2_front(g, slot)
        if s1:
            sums = stage1_cumsum(gate)
        if s2:
            stage2_back(g, slot, front)
        if s1:
            stage1_store(sums, 1 - slot)

    step(-1)
    step(0)
    if n_groups > 2:
        def body(g, carry):
            step(g)
            return carry

        lax.fori_loop(1, n_groups - 1, body, 0)
    if n_groups > 1:
        step(n_groups - 1)
    step(n_groups)


def _gla(proj, w2, ab, gn, *, batch, seq, lblk):
    t = batch * seq
    nsb = seq // lblk
    row_map = lambda cb: (lambda b, i: (b * nsb + i, cb))
    const = lambda b, i: (0, 0)
    n_chunks = lblk // CHUNK
    wide = next(w for w in (4, 2, 1) if n_chunks % w == 0)
    hand = lambda rows, dtype: pltpu.VMEM((2, wide, rows, GLA_KEY), dtype)
    return pl.pallas_call(
        functools.partial(_gla_kernel, n_chunks=n_chunks, wide=wide),
        grid=(batch, nsb),
        in_specs=[
            pl.BlockSpec((lblk, 2 * GLA_KEY), row_map(COL_GLA_QK // (2 * GLA_KEY))),
            pl.BlockSpec((lblk, GLA_VAL), row_map(COL_GLA_V // GLA_VAL)),
            pl.BlockSpec((lblk, GLA_VAL), row_map(COL_GLA_OG // GLA_VAL)),
            pl.BlockSpec((lblk, LANES), row_map(COL_GAL // LANES)),
            pl.BlockSpec((GLA_GATE_RANK, GLA_KEY), const),
            pl.BlockSpec((1, GLA_KEY), const),
            pl.BlockSpec((1, GLA_DV), const),
        ],
        out_specs=pl.BlockSpec((lblk, GLA_VAL), lambda b, i: (b * nsb + i, 0)),
        out_shape=jax.ShapeDtypeStruct((t, GLA_VAL), BF16),
        scratch_shapes=[
            pltpu.VMEM((GLA_HEADS, GLA_DV, GLA_DK), F32),
            hand(CHUNK, BF16), hand(CHUNK, BF16), hand(CHUNK, BF16), hand(CHUNK, BF16),
            hand(1, F32),
            pltpu.VMEM((2, wide, CHUNK, GLA_VAL), F32),
        ],
        compiler_params=pltpu.CompilerParams(
            dimension_semantics=("parallel", "arbitrary"), vmem_limit_bytes=VMEM_LIMIT),
        name="gla",
    )(proj, proj, proj, proj, w2, ab, gn)


def _pair_block_diag(y, lo):
    zero = jnp.zeros_like(y)
    return jnp.concatenate([jnp.where(lo, y, zero), jnp.where(lo, zero, y)], axis=0)


def _unit_lower_inverse(ns, masks, after_stage=None):
    after_stage = after_stage or {}
    stage = [0]

    def stage_done():
        hook = after_stage.get(stage[0])
        if hook is not None:
            hook()
        stage[0] += 1

    eye, blk16, off32, off64, lo = masks
    nd = [jnp.where(blk16, n, 0.0) for n in ns]
    x = [eye + d for d in nd]
    nd = [d.astype(BF16) for d in nd]
    m = [_mm(d, _pair_block_diag(d, lo)).astype(BF16) for d in nd]
    stage_done()
    c = ns[0].shape[0]
    for step in range(3):
        mb = [_pair_block_diag(mi, lo) for mi in m]
        if step < 2:
            xm = [_mm(jnp.concatenate([xi.astype(BF16), mi], axis=0), mbi) for xi, mi, mbi in zip(x, m, mb)]
            m = [r[c:].astype(BF16) for r in xm]
            xm = [r[:c] for r in xm]
        else:
            xm = [_mm(xi, mbi) for xi, mbi in zip(x, mb)]
        x = [xi + xmi for xi, xmi in zip(x, xm)]
        stage_done()
    for off in (off32, off64):
        xb = [xi.astype(BF16) for xi in x]
        xn = [_mm(xi, _pair_block_diag(jnp.where(off, n, 0.0).astype(BF16), lo)) for xi, n in zip(xb, ns)]
        stage_done()
        xnx = [_mm(a, _pair_block_diag(xi, lo)) for a, xi in zip(xn, xb)]
        x = [xi + b for xi, b in zip(x, xnx)]
        stage_done()
    return x


def _rwkv_kernel(r_ref, k_ref, v_ref, lo_ref, mu_r_ref, mu_k_ref, mu_v_ref, mu_lo_ref,
                 w0_ref, a0_ref, kk_ref, ka_ref, rk_ref, lnw_ref, lnb_ref,
                 ww2_ref, aw2_ref, gw2_ref, y_ref,
                 pr_ref, pk_ref, pv_ref, plo_ref, st_ref,
                 n_ref, prm_ref, ahl_ref, bk2_ref, rkv_ref, akv_ref, v2_ref, bonv_ref, gate_ref, elast_ref,
                 t_ref,
                 *, n_chunks, n_pairs):
    @pl.when(pl.program_id(2) == 0)
    def _():
        pr_ref[...] = jnp.zeros_like(pr_ref)
        pk_ref[...] = jnp.zeros_like(pk_ref)
        pv_ref[...] = jnp.zeros_like(pv_ref)
        plo_ref[...] = jnp.zeros_like(plo_ref)
        st_ref[...] = jnp.zeros_like(st_ref)

    width = n_pairs * LANES
    tri = (_iota((CHUNK, CHUNK), 0) >= _iota((CHUNK, CHUNK), 1)).astype(BF16)

    trow = _iota((CHUNK, LANES), 0)
    tcol = _iota((CHUNK, LANES), 1) & (RWKV_HEAD - 1)
    lo = _iota((CHUNK, LANES), 1) < RWKV_HEAD
    strict = trow > tcol
    incl = trow >= tcol
    blk16 = (trow // 16) == (tcol // 16)
    blk32 = (trow // 32) == (tcol // 32)
    inv_masks = ((trow == tcol).astype(F32), blk16, blk32 & (~blk16), ~blk32, lo)
    srow = _iota((LANES, LANES), 0)
    scol = _iota((LANES, LANES), 1)
    same = (srow < RWKV_HEAD) == (scol < RWKV_HEAD)
    first_w = _iota((CHUNK, width), 0) == 0
    first_lo = _iota((CHUNK, LORA_WIDTH), 0) == 0
    pairs = range(n_pairs)
    ls = [slice(p * LANES, (p + 1) * LANES) for p in pairs]

    def seg_sum(x):
        s0 = jnp.sum(jnp.where(lo, x, 0.0), axis=-1, keepdims=True)
        s1 = jnp.sum(jnp.where(lo, 0.0, x), axis=-1, keepdims=True)
        return jnp.where(lo, s0, s1)

    def stack(a, b):
        return jnp.concatenate([a, b], axis=0)

    def shifted(z, prev_ref, first):
        zs = jnp.where(first, prev_ref[...], pltpu.roll(z, 1, axis=0))
        prev_ref[...] = z[CHUNK - 1:CHUNK, :]
        return zs

    def stage1_lora(c):
        rows = _chunk_rows(c)
        zl = lo_ref[rows, :]
        lora = zl + (shifted(zl, plo_ref, first_lo) - zl) * mu_lo_ref[...]
        wl = lora[:, 0:DECAY_LORA]
        al = lora[:, DECAY_LORA:DECAY_LORA + AAA_LORA]
        gl = lora[:, DECAY_LORA + AAA_LORA:LORA_WIDTH]
        return (c, _mm(jnp.tanh(wl), ww2_ref[...]), _mm(al, aw2_ref[...]),
                _mm(_sigmoid(gl), gw2_ref[...]))

    def stage1_decay(lora_out):
        c, w_lin, a_lin, gate_all = lora_out
        logw_all = (-DECAY_SCALE) * _sigmoid(w0_ref[...] + w_lin)
        return c, logw_all, a_lin, gate_all, _cumsum_rows(tri, logw_all)

    def stage1_back(front, slot):
        c, logw_all, a_lin, gate_all, lw_all = front
        rows = _chunk_rows(c)
        zr = r_ref[rows, :]
        zk = k_ref[rows, :]
        zv = v_ref[rows, :]
        r_all = zr + (shifted(zr, pr_ref, first_w) - zr) * mu_r_ref[...]
        k_all = zk + (shifted(zk, pk_ref, first_w) - zk) * mu_k_ref[...]
        v_all = zv + (shifted(zv, pv_ref, first_w) - zv) * mu_v_ref[...]
        a_all = _sigmoid(a0_ref[...] + a_lin)

        r = [r_all[:, s] for s in ls]
        k = [k_all[:, s] for s in ls]
        v = [v_all[:, s] for s in ls]
        a = [a_all[:, s] for s in ls]
        lw = [lw_all[:, s] for s in ls]
        kk = [k[p] * kk_ref[:, ls[p]] for p in pairs]
        kkn = [kk[p] * lax.rsqrt(jnp.maximum(seg_sum(kk[p] * kk[p]), KK_NORM_FLOOR ** 2)) for p in pairs]
        kmod = [k[p] * (1.0 + (a[p] - 1.0) * ka_ref[:, ls[p]]) for p in pairs]
        bvec = [kkn[p] * a[p] for p in pairs]
        mid = [x[CHUNK // 2 - 1:CHUNK // 2, :] for x in lw]
        last = [x[CHUNK - 1:CHUNK, :] for x in lw]
        zero = jnp.zeros((CHUNK, LANES), BF16)
        e_in = [jnp.exp(lw[p] - mid[p]) for p in pairs]
        e_ng = [1.0 / e_in[p] for p in pairs]
        ag_f = [-kkn[p] * (e_in[p] * jnp.exp(-logw_all[:, ls[p]])) for p in pairs]
        rg_f = [r[p] * e_in[p] for p in pairs]
        bg_f = [bvec[p] * e_ng[p] for p in pairs]
        kg_f = [kmod[p] * e_ng[p] for p in pairs]
        ag = [x.astype(BF16) for x in ag_f]
        rg = [x.astype(BF16) for x in rg_f]
        v16 = [v[p].astype(BF16) for p in pairs]
        pm = [_mm_nt(jnp.concatenate([jnp.where(lo, ag[p], zero), jnp.where(lo, rg[p], zero),
                                      jnp.where(lo, zero, ag[p]), jnp.where(lo, zero, rg[p])], axis=0),
                     stack(bg_f[p].astype(BF16), kg_f[p].astype(BF16))) for p in pairs]
        pm0 = [x[0:2 * CHUNK] for x in pm]
        pm1 = [pltpu.roll(x[2 * CHUNK:4 * CHUNK], RWKV_HEAD, axis=1) for x in pm]
        ak = [jnp.where(strict, jnp.where(lo, pm1[p][0:CHUNK], pm0[p][0:CHUNK]), 0.0) for p in pairs]
        rk = [jnp.where(incl, jnp.where(lo, pm1[p][CHUNK:], pm0[p][CHUNK:]), 0.0) for p in pairs]
        vx = [stack(jnp.where(lo, zero, v16[p]), jnp.where(lo, v16[p], zero)) for p in pairs]
        akrk = [_mm(stack(ak[p], rk[p]), vx[p]) for p in pairs]
        for p in pairs:
            e_mid = jnp.exp(mid[p])
            e_tail = jnp.exp(last[p] - mid[p])
            n_ref[slot, p] = jnp.where(strict, jnp.where(lo, pm0[p][0:CHUNK], pm1[p][0:CHUNK]), 0.0)
            rb = jnp.where(incl, jnp.where(lo, pm0[p][CHUNK:], pm1[p][CHUNK:]), 0.0)
            prm_ref[slot, p] = rb.astype(BF16)
            akv_ref[slot, p] = akrk[p][0:CHUNK]
            rkv_ref[slot, p] = akrk[p][CHUNK:]
            ahl_ref[slot, p] = stack((ag_f[p] * e_mid).astype(BF16), (rg_f[p] * e_mid).astype(BF16))
            bk2_ref[slot, p] = stack((bg_f[p] * e_tail).astype(BF16), (kg_f[p] * e_tail).astype(BF16))
            v2_ref[slot, p] = v16[p]
            bonv_ref[slot, p] = seg_sum(r[p] * kmod[p] * rk_ref[:, ls[p]]) * v[p]
            gate_ref[slot, p] = gate_all[:, ls[p]]
            elast_ref[slot, p] = jnp.exp(last[p])

    def stage2(slot, tslot, after_stage=None):
        tinv = _unit_lower_inverse([n_ref[slot, p] for p in pairs], inv_masks, after_stage)
        for p in pairs:
            t_ref[tslot, p] = tinv[p].astype(BF16)

    def stage3_state(slot):
        st = [st_ref[p] for p in pairs]
        ah = [_mm_nt(ahl_ref[slot, p], st[p]) for p in pairs]
        return st, ah

    def stage3_solve(slot, tslot, front):
        st, ah = front
        w = [(ah[p][0:CHUNK] + akv_ref[slot, p]).astype(BF16) for p in pairs]
        return [_mm(t_ref[tslot, p], _pair_block_diag(w[p], lo)).astype(BF16) for p in pairs]

    def stage3_out(c, slot, front, u):
        st, ah = front
        rows = _chunk_rows(c)
        ys = [_mm(prm_ref[slot, p], _pair_block_diag(u[p], lo)) + rkv_ref[slot, p] for p in pairs]
        dh = [_mm_tn(stack(u[p], v2_ref[slot, p]), bk2_ref[slot, p]) for p in pairs]
        for p in pairs:
            st_ref[p] = st[p] * elast_ref[slot, p] + jnp.where(same, dh[p], 0.0)
            y = ah[p][CHUNK:] + ys[p]
            mean = seg_sum(y) * (1.0 / RWKV_HEAD)
            d = y - mean
            var = seg_sum(d * d) * (1.0 / RWKV_HEAD)
            yn = d * lax.rsqrt(var + RWKV_GN_EPS) * lnw_ref[:, ls[p]] + lnb_ref[:, ls[p]]
            y_ref[rows, ls[p]] = ((yn + bonv_ref[slot, p]) * gate_ref[slot, p]).astype(BF16)

    def step(c):
        s1, s2, s3 = [0 <= c + k < n_chunks if isinstance(c, int) else True for k in (2, 1, 0)]
        slot3, slot2, slot1 = c % 3, (c + 1) % 3, (c + 2) % 3
        tslot3, tslot2 = c & 1, (c + 1) & 1
        held = {}

        def after0():
            if s1:
                held["lora"] = stage1_lora(c + 2)
            if s3:
                held["u"] = stage3_solve(slot3, tslot3, held["state"])

        def after1():
            if s3:
                stage3_out(c, slot3, held["state"], held["u"])

        def after2():
            if s1:
                held["front"] = stage1_decay(held["lora"])

        if s3:
            held["state"] = stage3_state(slot3)
        if s2:
            stage2(slot2, tslot2, {0: after0, 1: after1, 2: after2})
        else:
            after0(), after1(), after2()
        if s1:
            stage1_back(held["front"], slot1)

    n_steady = max(n_chunks - 2, 0)
    for c in range(-2, 0):
        step(c)
    if n_steady > 0:
        def body(c, carry):
            step(c)
            return carry

        lax.fori_loop(0, n_steady, body, 0)
    for c in range(n_steady, n_chunks):
        step(c)


def _rwkv(proj, params, *, batch, seq, lblk, n_pairs):
    (mu_r, mu_k, mu_v, mu_lo, w0, a0, k_k, k_a, r_k, ln_w, ln_b, ww2, aw2, gw2) = params
    t = batch * seq
    nsb = seq // lblk
    width = n_pairs * LANES
    ngrp = RWKV_DIM // width
    row_map = lambda c0: (lambda b, g, i: (b * nsb + i, c0 // width + g))
    grp = lambda b, g, i: (0, g)
    const = lambda b, g, i: (0, 0)
    vec = pl.BlockSpec((1, width), grp)
    rec = lambda slots, rows, cols, dtype: pltpu.VMEM((slots, n_pairs, rows, cols), dtype)
    return pl.pallas_call(
        functools.partial(_rwkv_kernel, n_chunks=lblk // CHUNK, n_pairs=n_pairs),
        grid=(batch, ngrp, nsb),
        in_specs=[
            pl.BlockSpec((lblk, width), row_map(COL_R)),
            pl.BlockSpec((lblk, width), row_map(COL_K)),
            pl.BlockSpec((lblk, width), row_map(COL_V)),
            pl.BlockSpec((lblk, LORA_WIDTH), lambda b, g, i: (b * nsb + i, COL_LORA // LORA_WIDTH)),
            vec, vec, vec,
            pl.BlockSpec((1, LORA_WIDTH), const),
            vec, vec, vec, vec, vec, vec, vec,
            pl.BlockSpec((DECAY_LORA, width), grp),
            pl.BlockSpec((AAA_LORA, width), grp),
            pl.BlockSpec((GATE_LORA, width), grp),
        ],
        out_specs=pl.BlockSpec((lblk, width), lambda b, g, i: (b * nsb + i, g)),
        out_shape=jax.ShapeDtypeStruct((t, RWKV_DIM), BF16),
        scratch_shapes=[
            pltpu.VMEM((1, width), F32), pltpu.VMEM((1, width), F32), pltpu.VMEM((1, width), F32),
            pltpu.VMEM((1, LORA_WIDTH), F32),
            pltpu.VMEM((n_pairs, LANES, LANES), F32),
            rec(3, CHUNK, LANES, F32),
            rec(3, CHUNK, LANES, BF16),
            rec(3, LANES, LANES, BF16),
            rec(3, LANES, LANES, BF16),
            rec(3, CHUNK, LANES, F32),
            rec(3, CHUNK, LANES, F32),
            rec(3, CHUNK, LANES, BF16),
            rec(3, CHUNK, LANES, F32),
            rec(3, CHUNK, LANES, F32),
            rec(3, 1, LANES, F32),
            rec(2, CHUNK, LANES, BF16),
        ],
        compiler_params=pltpu.CompilerParams(
            dimension_semantics=("parallel", "parallel", "arbitrary"), vmem_limit_bytes=VMEM_LIMIT),
        name="rwkv",
    )(proj, proj, proj, proj, mu_r, mu_k, mu_v, mu_lo, w0, a0, k_k, k_a, r_k, ln_w, ln_b,
      ww2, aw2, gw2)


def _merge_kernel(x_ref, yg_ref, yr_ref, ga_ref, gb_ref, wg_ref, wr_ref, wo_ref, o_ref):
    mixed = (_sigmoid(ga_ref[...]) * jnp.dot(yg_ref[...], wg_ref[...], preferred_element_type=F32)
             + _sigmoid(gb_ref[...]) * jnp.dot(yr_ref[...], wr_ref[...], preferred_element_type=F32))
    o_ref[...] = x_ref[...] + _mm(mixed, wo_ref[...])


def _merge(x2, yg, yr, proj, wg, wr, wo, *, tm):
    t = x2.shape[0]
    tok = lambda i: (i, 0)
    const = lambda i: (0, 0)
    wspec = pl.BlockSpec((D_MODEL, D_MODEL), const)
    return pl.pallas_call(
        _merge_kernel,
        grid=(t // tm,),
        in_specs=[
            pl.BlockSpec((tm, D_MODEL), tok),
            pl.BlockSpec((tm, D_MODEL), tok),
            pl.BlockSpec((tm, D_MODEL), tok),
            pl.BlockSpec((tm, D_MODEL), lambda i: (i, COL_GATE // D_MODEL)),
            pl.BlockSpec((tm, D_MODEL), lambda i: (i, COL_GATE // D_MODEL + 1)),
            wspec, wspec, wspec,
        ],
        out_specs=pl.BlockSpec((tm, D_MODEL), tok),
        out_shape=jax.ShapeDtypeStruct((t, D_MODEL), F32),
        compiler_params=pltpu.CompilerParams(
            dimension_semantics=("parallel",), vmem_limit_bytes=VMEM_LIMIT),
        name="merge",
    )(x2, yg, yr, proj, proj, wg, wr, wo)


def _mlp_kernel(x_ref, g_ref, wu_ref, wd_ref, gf_ref, o_ref, *, final_norm):
    x = x_ref[...]
    h = (x * lax.rsqrt(jnp.mean(x * x, axis=-1, keepdims=True) + NORM_EPS) * g_ref[...]).astype(BF16)
    up = jnp.dot(h, wu_ref[...], preferred_element_type=F32)
    act = jnp.square(jnp.maximum(up, 0.0)).astype(BF16)
    x = x + jnp.dot(act, wd_ref[...], preferred_element_type=F32)
    if final_norm:
        x = x * lax.rsqrt(jnp.mean(x * x, axis=-1, keepdims=True) + NORM_EPS) * gf_ref[...]
    o_ref[...] = x


def _mlp(x2, g, wu, wd, gf, *, tm, final_norm):
    t = x2.shape[0]
    tok = lambda i: (i, 0)
    const = lambda i: (0, 0)
    return pl.pallas_call(
        functools.partial(_mlp_kernel, final_norm=final_norm),
        grid=(t // tm,),
        in_specs=[
            pl.BlockSpec((tm, D_MODEL), tok),
            pl.BlockSpec((1, D_MODEL), const),
            pl.BlockSpec((D_MODEL, D_FF), const),
            pl.BlockSpec((D_FF, D_MODEL), const),
            pl.BlockSpec((1, D_MODEL), const),
        ],
        out_specs=pl.BlockSpec((tm, D_MODEL), tok),
        out_shape=jax.ShapeDtypeStruct((t, D_MODEL), F32),
        compiler_params=pltpu.CompilerParams(
            dimension_semantics=("parallel",), vmem_limit_bytes=VMEM_LIMIT),
        name="mlp",
    )(x2, g, wu, wd, gf)


def _regroup_w_in(w):
    w = w.astype(BF16)
    gla_w = w[:, :GLA_WIDTH]
    rw = w[:, GLA_WIDTH:GLA_WIDTH + RWKV_WIDTH]
    gates = w[:, GLA_WIDTH + RWKV_WIDTH:]
    gla_main = GLA_WIDTH - GLA_GATE_RANK
    pad = jnp.zeros((w.shape[0], PROJ_WIDTH - COL_GAL - GLA_GATE_RANK), w.dtype)
    return jnp.concatenate([gates, gla_w[:, :gla_main], rw, gla_w[:, gla_main:], pad], axis=1)


def _row(v):
    return v.reshape(1, -1).astype(F32)


def _largest_divisor(n, candidates):
    return next(c for c in candidates if n % c == 0)


def _tile_plan(batch, seq):
    t = batch * seq
    return dict(
        tm=_largest_divisor(t, (TOKEN_TILE, CHUNK)),
        tn=PROJ_WIDTH // 2,
        lblk=_largest_divisor(seq, (SEQ_BLOCK, CHUNK)),
    )


def kernel(x, norm_mix, w_in, gla_a_w2, gla_a_b, gla_norm, rwkv_mu, rwkv_w0, rwkv_w_w2, rwkv_a0, rwkv_a_w2, rwkv_g_w2, rwkv_k_k, rwkv_k_a, rwkv_r_k, rwkv_ln_w, rwkv_ln_b, w_branch_gla, w_branch_rwkv, w_out, norm_mlp, w_up, w_down, norm_final):
    batch, seq, d = x.shape
    assert d == D_MODEL and seq % CHUNK == 0
    t = batch * seq
    depth = norm_mix.shape[0]
    plan = _tile_plan(batch, seq)
    tm, tn, lblk = plan["tm"], plan["tn"], plan["lblk"]
    x2 = x.reshape(t, d)
    for l in range(depth):
        wp = _regroup_w_in(w_in[l]).reshape(D_MODEL, PROJ_WIDTH // tn, tn).transpose(1, 0, 2)
        proj = _inproj(x2, _row(norm_mix[l]), wp, tm=tm, tn=tn)

        y_gla = _gla(proj, gla_a_w2[l].astype(BF16), _row(gla_a_b[l]), _row(gla_norm[l]),
                     batch=batch, seq=seq, lblk=lblk)

        mu = rwkv_mu[l]
        params = (_row(mu[0:RWKV_DIM]), _row(mu[RWKV_DIM:2 * RWKV_DIM]), _row(mu[2 * RWKV_DIM:3 * RWKV_DIM]),
                  _row(mu[3 * RWKV_DIM:]), _row(rwkv_w0[l]), _row(rwkv_a0[l]), _row(rwkv_k_k[l]),
                  _row(rwkv_k_a[l]), _row(rwkv_r_k[l]), _row(rwkv_ln_w[l]), _row(rwkv_ln_b[l]),
                  rwkv_w_w2[l].astype(BF16), rwkv_a_w2[l].astype(BF16), rwkv_g_w2[l].astype(BF16))
        y_rwkv = _rwkv(proj, params, batch=batch, seq=seq, lblk=lblk, n_pairs=RWKV_DIM // LANES)

        x2 = _merge(x2, y_gla, y_rwkv, proj, w_branch_gla[l].astype(BF16), w_branch_rwkv[l].astype(BF16),
                    w_out[l].astype(BF16), tm=tm)
        x2 = _mlp(x2, _row(norm_mlp[l]), w_up[l].astype(BF16), w_down[l].astype(BF16), _row(norm_final),
                  tm=tm, final_norm=(l == depth - 1))
    return x2.reshape(batch, seq, d)
```

```python
import functools

import jax
import jax.numpy as jnp
from jax import lax
from jax.experimental import pallas as pl
from jax.experimental.pallas import tpu as pltpu

F32 = jnp.float32
BF16 = jnp.bfloat16

D_MODEL = 1024
GLA_HEADS = 4
GLA_KEY = D_MODEL // 2
GLA_VAL = D_MODEL
GLA_DK = GLA_KEY // GLA_HEADS
GLA_DV = GLA_VAL // GLA_HEADS
GLA_GATE_RANK = 16
GLA_GATE_NORM = 16.0
GLA_NORM_EPS = 1e-5
RWKV_HEAD = 64
RWKV_DIM = D_MODEL
DECAY_LORA = 64
AAA_LORA = 64
GATE_LORA = 128
RWKV_GN_EPS = 64e-5
KK_NORM_FLOOR = 1e-12
DECAY_SCALE = 0.6065306597126334
D_FF = 4 * D_MODEL
NORM_EPS = 1e-6

GLA_WIDTH = 2 * GLA_KEY + 2 * GLA_VAL + GLA_GATE_RANK
RWKV_WIDTH = 3 * RWKV_DIM + DECAY_LORA + AAA_LORA + GATE_LORA

LANES = 128
MXU_COLS = 256
CHUNK = 64
TOKEN_TILE = 512
SEQ_BLOCK = 1024
LORA_WIDTH = DECAY_LORA + AAA_LORA + GATE_LORA
COL_GATE = 0
COL_GLA_QK = COL_GATE + 2 * D_MODEL
COL_GLA_V = COL_GLA_QK + 2 * GLA_KEY
COL_GLA_OG = COL_GLA_V + GLA_VAL
COL_R = COL_GLA_OG + GLA_VAL
COL_K = COL_R + RWKV_DIM
COL_V = COL_K + RWKV_DIM
COL_LORA = COL_V + RWKV_DIM
COL_GAL = COL_LORA + LORA_WIDTH
PROJ_WIDTH = -(-(COL_GAL + GLA_GATE_RANK) // (2 * MXU_COLS)) * (2 * MXU_COLS)

VMEM_CAPACITY = 64 * 1024 * 1024
VMEM_LIMIT = VMEM_CAPACITY * 7 // 8


def _mm(a, b):
    return jnp.dot(a.astype(BF16), b.astype(BF16), preferred_element_type=F32)


def _mm_nt(a, b):
    return lax.dot_general(a.astype(BF16), b.astype(BF16), (((1,), (1,)), ((), ())),
                           preferred_element_type=F32)


def _mm_tn(a, b):
    return lax.dot_general(a.astype(BF16), b.astype(BF16), (((0,), (0,)), ((), ())),
                           preferred_element_type=F32)


def _cumsum_rows(tri, x):
    tri = tri.astype(BF16)
    hi = x.astype(BF16)
    lo = (x - hi.astype(F32)).astype(BF16)
    return jnp.dot(jnp.concatenate([tri, tri], axis=1), jnp.concatenate([hi, lo], axis=0),
                   preferred_element_type=F32)


def _softplus(x):
    return jnp.maximum(x, 0.0) + jnp.log(1.0 + jnp.exp(-jnp.abs(x)))


def _sigmoid(x):
    return 1.0 / (1.0 + jnp.exp(-x))


def _iota(shape, dim):
    return lax.broadcasted_iota(jnp.int32, shape, dim)


def _chunk_rows(c):
    if isinstance(c, int):
        return pl.ds(c * CHUNK, CHUNK)
    return pl.ds(pl.multiple_of(c * CHUNK, CHUNK), CHUNK)


def _inproj_kernel(x_ref, g_ref, w_ref, o_ref, h_ref):
    @pl.when(pl.program_id(1) == 0)
    def _():
        x = x_ref[...]
        ms = jnp.mean(x * x, axis=-1, keepdims=True)
        h_ref[...] = (x * lax.rsqrt(ms + NORM_EPS) * g_ref[...]).astype(BF16)

    o_ref[...] = jnp.dot(h_ref[...], w_ref[pl.program_id(1)], preferred_element_type=F32)


def _inproj(x2, g, wp, *, tm, tn):
    t = x2.shape[0]
    return pl.pallas_call(
        _inproj_kernel,
        grid=(t // tm, PROJ_WIDTH // tn),
        in_specs=[
            pl.BlockSpec((tm, D_MODEL), lambda i, j: (i, 0)),
            pl.BlockSpec((1, D_MODEL), lambda i, j: (0, 0)),
            pl.BlockSpec((PROJ_WIDTH // tn, D_MODEL, tn), lambda i, j: (0, 0, 0),
                         pipeline_mode=pl.Buffered(1)),
        ],
        out_specs=pl.BlockSpec((tm, tn), lambda i, j: (i, j)),
        out_shape=jax.ShapeDtypeStruct((t, PROJ_WIDTH), F32),
        scratch_shapes=[pltpu.VMEM((tm, D_MODEL), BF16)],
        compiler_params=pltpu.CompilerParams(
            dimension_semantics=("parallel", "arbitrary"), vmem_limit_bytes=VMEM_LIMIT),
        name="inproj",
    )(x2, g, wp)


def _gla_kernel(qk_ref, v_ref, og_ref, gal_ref, w2_ref, ab_ref, gn_ref, y_ref, st_ref,
                qi_ref, ki_ref, qd_ref, kd_ref, dec_ref, o_ref, *, n_chunks, wide):
    @pl.when(pl.program_id(1) == 0)
    def _():
        st_ref[...] = jnp.zeros_like(st_ref)

    row = _iota((CHUNK, CHUNK), 0)
    col = _iota((CHUNK, CHUNK), 1)
    causal = row >= col
    tri = causal.astype(BF16)
    w2 = w2_ref[...]
    ab = ab_ref[...]
    gn = gn_ref[...]
    heads = range(GLA_HEADS)
    ks = [slice(h * GLA_DK, (h + 1) * GLA_DK) for h in heads]
    vs = [slice(h * GLA_DV, (h + 1) * GLA_DV) for h in heads]

    group = range(wide)
    n_groups = n_chunks // wide

    def rows_of(g, j):
        return _chunk_rows(g * wide + j)

    def stage1_gate(g):
        return g, [_mm(gal_ref[rows_of(g, j), 0:GLA_GATE_RANK], w2) for j in group]

    def stage1_cumsum(gate):
        g, lins = gate
        las = [-_softplus(-(lin + ab)) * (1.0 / GLA_GATE_NORM) for lin in lins]
        return g, [_cumsum_rows(tri, la) for la in las]

    def stage1_store(cs, slot):
        g, bs = cs
        for j in group:
            b = bs[j]
            rows = rows_of(g, j)
            b_mid = b[CHUNK // 2 - 1:CHUNK // 2, :]
            b_last = b[CHUNK - 1:CHUNK, :]
            q = qk_ref[rows, 0:GLA_KEY] * (GLA_DK ** -0.5)
            k = qk_ref[rows, GLA_KEY:2 * GLA_KEY]
            e_in = jnp.exp(b - b_mid)
            qi = q * e_in
            ki = k * (1.0 / e_in)
            qi_ref[slot, j] = qi.astype(BF16)
            ki_ref[slot, j] = ki.astype(BF16)
            qd_ref[slot, j] = (qi * jnp.exp(b_mid)).astype(BF16)
            kd_ref[slot, j] = (ki * jnp.exp(b_last - b_mid)).astype(BF16)
            dec_ref[slot, j] = jnp.exp(b_last)

    def stage2_front(g, slot):
        v = [[v_ref[rows_of(g, j), vs[h]].astype(BF16) for h in heads] for j in group]
        st = [st_ref[h] for h in heads]
        s = [[_mm_nt(qi_ref[slot, j, :, ks[h]], ki_ref[slot, j, :, ks[h]]) for h in heads] for j in group]
        upd = [[_mm_tn(v[j][h], kd_ref[slot, j, :, ks[h]]) for h in heads] for j in group]
        inter0 = [_mm_nt(qd_ref[slot, 0, :, ks[h]], st[h]) for h in heads]
        return v, st, s, upd, inter0

    def stage2_back(g, slot, front):
        v, st, s, upd, inter0 = front
        intra = [[_mm(jnp.where(causal, s[j][h], 0.0), v[j][h]) for h in heads] for j in group]
        inter = [inter0]
        for j in group:
            st = [st[h] * dec_ref[slot, j, :, ks[h]] + upd[j][h] for h in heads]
            if j + 1 < wide:
                inter.append([_mm_nt(qd_ref[slot, j + 1, :, ks[h]], st[h]) for h in heads])
        for h in heads:
            st_ref[h] = st[h]
        for j in group:
            for h in heads:
                o_ref[slot, j, :, vs[h]] = intra[j][h] + inter[j][h]

    def stage3(g, slot):
        for j in group:
            rows = rows_of(g, j)
            for h in heads:
                o = o_ref[slot, j, :, vs[h]]
                o = o * lax.rsqrt(jnp.mean(o * o, axis=-1, keepdims=True) + GLA_NORM_EPS) * gn
                og = og_ref[rows, vs[h]]
                y_ref[rows, vs[h]] = (o * (og * _sigmoid(og))).astype(BF16)

    def step(g):
        s1, s2, s3 = [0 <= g + k < n_groups if isinstance(g, int) else True for k in (1, 0, -1)]
        slot = g & 1
        if s3:
            stage3(g - 1, 1 - slot)
        if s1:
            gate = stage1_gate(g + 1)
        if s2:
            front = stage2_front(g, slot)
        if s1:
            sums = stage1_cumsum(gate)
        if s2:
            stage2_back(g, slot, front)
        if s1:
            stage1_store(sums, 1 - slot)

    step(-1)
    step(0)
    if n_groups > 2:
        def body(g, carry):
            step(g)
            return carry

        lax.fori_loop(1, n_groups - 1, body, 0)
    if n_groups > 1:
        step(n_groups - 1)
    step(n_groups)


def _gla(proj, w2, ab, gn, *, batch, seq, lblk):
    t = batch * seq
    nsb = seq // lblk
    row_map = lambda cb: (lambda b, i: (b * nsb + i, cb))
    const = lambda b, i: (0, 0)
    n_chunks = lblk // CHUNK
    wide = next(w for w in (8, 4, 2, 1) if n_chunks % w == 0)
    hand = lambda rows, dtype: pltpu.VMEM((2, wide, rows, GLA_KEY), dtype)
    return pl.pallas_call(
        functools.partial(_gla_kernel, n_chunks=n_chunks, wide=wide),
        grid=(batch, nsb),
        in_specs=[
            pl.BlockSpec((lblk, 2 * GLA_KEY), row_map(COL_GLA_QK // (2 * GLA_KEY))),
            pl.BlockSpec((lblk, GLA_VAL), row_map(COL_GLA_V // GLA_VAL)),
            pl.BlockSpec((lblk, GLA_VAL), row_map(COL_GLA_OG // GLA_VAL)),
            pl.BlockSpec((lblk, LANES), row_map(COL_GAL // LANES)),
            pl.BlockSpec((GLA_GATE_RANK, GLA_KEY), const),
            pl.BlockSpec((1, GLA_KEY), const),
            pl.BlockSpec((1, GLA_DV), const),
        ],
        out_specs=pl.BlockSpec((lblk, GLA_VAL), lambda b, i: (b * nsb + i, 0)),
        out_shape=jax.ShapeDtypeStruct((t, GLA_VAL), BF16),
        scratch_shapes=[
            pltpu.VMEM((GLA_HEADS, GLA_DV, GLA_DK), F32),
            hand(CHUNK, BF16), hand(CHUNK, BF16), hand(CHUNK, BF16), hand(CHUNK, BF16),
            hand(1, F32),
            pltpu.VMEM((2, wide, CHUNK, GLA_VAL), F32),
        ],
        compiler_params=pltpu.CompilerParams(
            dimension_semantics=("parallel", "arbitrary"), vmem_limit_bytes=VMEM_LIMIT),
        name="gla",
    )(proj, proj, proj, proj, w2, ab, gn)


def _pair_block_diag(y, lo):
    zero = jnp.zeros_like(y)
    return jnp.concatenate([jnp.where(lo, y, zero), jnp.where(lo, zero, y)], axis=0)


def _unit_lower_inverse(ns, masks, after_stage=None):
    after_stage = after_stage or {}
    stage = [0]

    def stage_done():
        hook = after_stage.get(stage[0])
        if hook is not None:
            hook()
        stage[0] += 1

    eye, blk16, off32, off64, lo = masks
    nd = [jnp.where(blk16, n, 0.0) for n in ns]
    x = [eye + d for d in nd]
    nd = [d.astype(BF16) for d in nd]
    m = [_mm(d, _pair_block_diag(d, lo)).astype(BF16) for d in nd]
    stage_done()
    c = ns[0].shape[0]
    for step in range(3):
        mb = [_pair_block_diag(mi, lo) for mi in m]
        if step < 2:
            xm = [_mm(jnp.concatenate([xi.astype(BF16), mi], axis=0), mbi) for xi, mi, mbi in zip(x, m, mb)]
            m = [r[c:].astype(BF16) for r in xm]
            xm = [r[:c] for r in xm]
        else:
            xm = [_mm(xi, mbi) for xi, mbi in zip(x, mb)]
        x = [xi + xmi for xi, xmi in zip(x, xm)]
        stage_done()
    for off in (off32, off64):
        xb = [xi.astype(BF16) for xi in x]
        xn = [_mm(xi, _pair_block_diag(jnp.where(off, n, 0.0).astype(BF16), lo)) for xi, n in zip(xb, ns)]
        stage_done()
        xnx = [_mm(a, _pair_block_diag(xi, lo)) for a, xi in zip(xn, xb)]
        x = [xi + b for xi, b in zip(x, xnx)]
        stage_done()
    return x


def _rwkv_kernel(r_ref, k_ref, v_ref, lo_ref, mu_r_ref, mu_k_ref, mu_v_ref, mu_lo_ref,
                 w0_ref, a0_ref, kk_ref, ka_ref, rk_ref, lnw_ref, lnb_ref,
                 ww2_ref, aw2_ref, gw2_ref, y_ref,
                 pr_ref, pk_ref, pv_ref, plo_ref, st_ref,
                 n_ref, prm_ref, ahl_ref, bk2_ref, rkv_ref, akv_ref, v2_ref, bonv_ref, gate_ref, elast_ref,
                 t_ref,
                 *, n_chunks, n_pairs):
    @pl.when(pl.program_id(2) == 0)
    def _():
        pr_ref[...] = jnp.zeros_like(pr_ref)
        pk_ref[...] = jnp.zeros_like(pk_ref)
        pv_ref[...] = jnp.zeros_like(pv_ref)
        plo_ref[...] = jnp.zeros_like(plo_ref)
        st_ref[...] = jnp.zeros_like(st_ref)

    width = n_pairs * LANES
    tri = (_iota((CHUNK, CHUNK), 0) >= _iota((CHUNK, CHUNK), 1)).astype(BF16)

    trow = _iota((CHUNK, LANES), 0)
    tcol = _iota((CHUNK, LANES), 1) & (RWKV_HEAD - 1)
    lo = _iota((CHUNK, LANES), 1) < RWKV_HEAD
    strict = trow > tcol
    incl = trow >= tcol
    blk16 = (trow // 16) == (tcol // 16)
    blk32 = (trow // 32) == (tcol // 32)
    inv_masks = ((trow == tcol).astype(F32), blk16, blk32 & (~blk16), ~blk32, lo)
    srow = _iota((LANES, LANES), 0)
    scol = _iota((LANES, LANES), 1)
    same = (srow < RWKV_HEAD) == (scol < RWKV_HEAD)
    first_w = _iota((CHUNK, width), 0) == 0
    first_lo = _iota((CHUNK, LORA_WIDTH), 0) == 0
    pairs = range(n_pairs)
    ls = [slice(p * LANES, (p + 1) * LANES) for p in pairs]

    def seg_sum(x):
        s0 = jnp.sum(jnp.where(lo, x, 0.0), axis=-1, keepdims=True)
        s1 = jnp.sum(jnp.where(lo, 0.0, x), axis=-1, keepdims=True)
        return jnp.where(lo, s0, s1)

    def stack(a, b):
        return jnp.concatenate([a, b], axis=0)

    def shifted(z, prev_ref, first):
        zs = jnp.where(first, prev_ref[...], pltpu.roll(z, 1, axis=0))
        prev_ref[...] = z[CHUNK - 1:CHUNK, :]
        return zs

    def stage1_lora(c):
        rows = _chunk_rows(c)
        zl = lo_ref[rows, :]
        lora = zl + (shifted(zl, plo_ref, first_lo) - zl) * mu_lo_ref[...]
        wl = lora[:, 0:DECAY_LORA]
        al = lora[:, DECAY_LORA:DECAY_LORA + AAA_LORA]
        gl = lora[:, DECAY_LORA + AAA_LORA:LORA_WIDTH]
        return (c, _mm(jnp.tanh(wl), ww2_ref[...]), _mm(al, aw2_ref[...]),
                _mm(_sigmoid(gl), gw2_ref[...]))

    def stage1_decay(lora_out):
        c, w_lin, a_lin, gate_all = lora_out
        logw_all = (-DECAY_SCALE) * _sigmoid(w0_ref[...] + w_lin)
        return c, logw_all, a_lin, gate_all, _cumsum_rows(tri, logw_all)

    def stage1_back(front, slot):
        c, logw_all, a_lin, gate_all, lw_all = front
        rows = _chunk_rows(c)
        zr = r_ref[rows, :]
        zk = k_ref[rows, :]
        zv = v_ref[rows, :]
        r_all = zr + (shifted(zr, pr_ref, first_w) - zr) * mu_r_ref[...]
        k_all = zk + (shifted(zk, pk_ref, first_w) - zk) * mu_k_ref[...]
        v_all = zv + (shifted(zv, pv_ref, first_w) - zv) * mu_v_ref[...]
        a_all = _sigmoid(a0_ref[...] + a_lin)

        r = [r_all[:, s] for s in ls]
        k = [k_all[:, s] for s in ls]
        v = [v_all[:, s] for s in ls]
        a = [a_all[:, s] for s in ls]
        lw = [lw_all[:, s] for s in ls]
        kk = [k[p] * kk_ref[:, ls[p]] for p in pairs]
        kkn = [kk[p] * lax.rsqrt(jnp.maximum(seg_sum(kk[p] * kk[p]), KK_NORM_FLOOR ** 2)) for p in pairs]
        kmod = [k[p] * (1.0 + (a[p] - 1.0) * ka_ref[:, ls[p]]) for p in pairs]
        bvec = [kkn[p] * a[p] for p in pairs]
        mid = [x[CHUNK // 2 - 1:CHUNK // 2, :] for x in lw]
        last = [x[CHUNK - 1:CHUNK, :] for x in lw]
        zero = jnp.zeros((CHUNK, LANES), BF16)
        e_in = [jnp.exp(lw[p] - mid[p]) for p in pairs]
        e_ng = [1.0 / e_in[p] for p in pairs]
        ag_f = [-kkn[p] * (e_in[p] * jnp.exp(-logw_all[:, ls[p]])) for p in pairs]
        rg_f = [r[p] * e_in[p] for p in pairs]
        bg_f = [bvec[p] * e_ng[p] for p in pairs]
        kg_f = [kmod[p] * e_ng[p] for p in pairs]
        ag = [x.astype(BF16) for x in ag_f]
        rg = [x.astype(BF16) for x in rg_f]
        v16 = [v[p].astype(BF16) for p in pairs]
        pm = [_mm_nt(jnp.concatenate([jnp.where(lo, ag[p], zero), jnp.where(lo, rg[p], zero),
                                      jnp.where(lo, zero, ag[p]), jnp.where(lo, zero, rg[p])], axis=0),
                     stack(bg_f[p].astype(BF16), kg_f[p].astype(BF16))) for p in pairs]
        pm0 = [x[0:2 * CHUNK] for x in pm]
        pm1 = [pltpu.roll(x[2 * CHUNK:4 * CHUNK], RWKV_HEAD, axis=1) for x in pm]
        ak = [jnp.where(strict, jnp.where(lo, pm1[p][0:CHUNK], pm0[p][0:CHUNK]), 0.0) for p in pairs]
        rk = [jnp.where(incl, jnp.where(lo, pm1[p][CHUNK:], pm0[p][CHUNK:]), 0.0) for p in pairs]
        vx = [stack(jnp.where(lo, zero, v16[p]), jnp.where(lo, v16[p], zero)) for p in pairs]
        akrk = [_mm(stack(ak[p], rk[p]), vx[p]) for p in pairs]
        for p in pairs:
            e_mid = jnp.exp(mid[p])
            e_tail = jnp.exp(last[p] - mid[p])
            n_ref[slot, p] = jnp.where(strict, jnp.where(lo, pm0[p][0:CHUNK], pm1[p][0:CHUNK]), 0.0)
            rb = jnp.where(incl, jnp.where(lo, pm0[p][CHUNK:], pm1[p][CHUNK:]), 0.0)
            prm_ref[slot, p] = rb.astype(BF16)
            akv_ref[slot, p] = akrk[p][0:CHUNK]
            rkv_ref[slot, p] = akrk[p][CHUNK:]
            ahl_ref[slot, p] = stack((ag_f[p] * e_mid).astype(BF16), (rg_f[p] * e_mid).astype(BF16))
            bk2_ref[slot, p] = stack((bg_f[p] * e_tail).astype(BF16), (kg_f[p] * e_tail).astype(BF16))
            v2_ref[slot, p] = v16[p]
            bonv_ref[slot, p] = seg_sum(r[p] * kmod[p] * rk_ref[:, ls[p]]) * v[p]
            gate_ref[slot, p] = gate_all[:, ls[p]]
            elast_ref[slot, p] = jnp.exp(last[p])

    def stage2(slot, tslot, after_stage=None):
        tinv = _unit_lower_inverse([n_ref[slot, p] for p in pairs], inv_masks, after_stage)
        for p in pairs:
            t_ref[tslot, p] = tinv[p].astype(BF16)

    def stage3_state(slot):
        st = [st_ref[p] for p in pairs]
        ah = [_mm_nt(ahl_ref[slot, p], st[p]) for p in pairs]
        return st, ah

    def stage3_solve(slot, tslot, front):
        st, ah = front
        w = [(ah[p][0:CHUNK] + akv_ref[slot, p]).astype(BF16) for p in pairs]
        return [_mm(t_ref[tslot, p], _pair_block_diag(w[p], lo)).astype(BF16) for p in pairs]

    def stage3_out(c, slot, front, u):
        st, ah = front
        rows = _chunk_rows(c)
        ys = [_mm(prm_ref[slot, p], _pair_block_diag(u[p], lo)) + rkv_ref[slot, p] for p in pairs]
        dh = [_mm_tn(stack(u[p], v2_ref[slot, p]), bk2_ref[slot, p]) for p in pairs]
        for p in pairs:
            st_ref[p] = st[p] * elast_ref[slot, p] + jnp.where(same, dh[p], 0.0)
            y = ah[p][CHUNK:] + ys[p]
            mean = seg_sum(y) * (1.0 / RWKV_HEAD)
            d = y - mean
            var = seg_sum(d * d) * (1.0 / RWKV_HEAD)
            yn = d * lax.rsqrt(var + RWKV_GN_EPS) * lnw_ref[:, ls[p]] + lnb_ref[:, ls[p]]
            y_ref[rows, ls[p]] = ((yn + bonv_ref[slot, p]) * gate_ref[slot, p]).astype(BF16)

    def step(c):
        s1, s2, s3 = [0 <= c + k < n_chunks if isinstance(c, int) else True for k in (2, 1, 0)]
        slot3, slot2, slot1 = c % 3, (c + 1) % 3, (c + 2) % 3
        tslot3, tslot2 = c & 1, (c + 1) & 1
        held = {}

        def after0():
            if s1:
                held["lora"] = stage1_lora(c + 2)
            if s3:
                held["u"] = stage3_solve(slot3, tslot3, held["state"])

        def after1():
            if s3:
                stage3_out(c, slot3, held["state"], held["u"])

        def after2():
            if s1:
                held["front"] = stage1_decay(held["lora"])

        if s3:
            held["state"] = stage3_state(slot3)
        if s2:
            stage2(slot2, tslot2, {0: after0, 1: after1, 2: after2})
        else:
            after0(), after1(), after2()
        if s1:
            stage1_back(held["front"], slot1)

    n_steady = max(n_chunks - 2, 0)
    for c in range(-2, 0):
        step(c)
    if n_steady > 0:
        def body(c, carry):
            step(c)
            return carry

        lax.fori_loop(0, n_steady, body, 0)
    for c in range(n_steady, n_chunks):
        step(c)


def _rwkv(proj, params, *, batch, seq, lblk, n_pairs):
    (mu_r, mu_k, mu_v, mu_lo, w0, a0, k_k, k_a, r_k, ln_w, ln_b, ww2, aw2, gw2) = params
    t = batch * seq
    nsb = seq // lblk
    width = n_pairs * LANES
    ngrp = RWKV_DIM // width
    row_map = lambda c0: (lambda b, g, i: (b * nsb + i, c0 // width + g))
    grp = lambda b, g, i: (0, g)
    const = lambda b, g, i: (0, 0)
    vec = pl.BlockSpec((1, width), grp)
    rec = lambda slots, rows, cols, dtype: pltpu.VMEM((slots, n_pairs, rows, cols), dtype)
    return pl.pallas_call(
        functools.partial(_rwkv_kernel, n_chunks=lblk // CHUNK, n_pairs=n_pairs),
        grid=(batch, ngrp, nsb),
        in_specs=[
            pl.BlockSpec((lblk, width), row_map(COL_R)),
            pl.BlockSpec((lblk, width), row_map(COL_K)),
            pl.BlockSpec((lblk, width), row_map(COL_V)),
            pl.BlockSpec((lblk, LORA_WIDTH), lambda b, g, i: (b * nsb + i, COL_LORA // LORA_WIDTH)),
            vec, vec, vec,
            pl.BlockSpec((1, LORA_WIDTH), const),
            vec, vec, vec, vec, vec, vec, vec,
            pl.BlockSpec((DECAY_LORA, width), grp),
            pl.BlockSpec((AAA_LORA, width), grp),
            pl.BlockSpec((GATE_LORA, width), grp),
        ],
        out_specs=pl.BlockSpec((lblk, width), lambda b, g, i: (b * nsb + i, g)),
        out_shape=jax.ShapeDtypeStruct((t, RWKV_DIM), BF16),
        scratch_shapes=[
            pltpu.VMEM((1, width), F32), pltpu.VMEM((1, width), F32), pltpu.VMEM((1, width), F32),
            pltpu.VMEM((1, LORA_WIDTH), F32),
            pltpu.VMEM((n_pairs, LANES, LANES), F32),
            rec(3, CHUNK, LANES, F32),
            rec(3, CHUNK, LANES, BF16),
            rec(3, LANES, LANES, BF16),
            rec(3, LANES, LANES, BF16),
            rec(3, CHUNK, LANES, F32),
            rec(3, CHUNK, LANES, F32),
            rec(3, CHUNK, LANES, BF16),
            rec(3, CHUNK, LANES, F32),
            rec(3, CHUNK, LANES, F32),
            rec(3, 1, LANES, F32),
            rec(2, CHUNK, LANES, BF16),
        ],
        compiler_params=pltpu.CompilerParams(
            dimension_semantics=("parallel", "parallel", "arbitrary"), vmem_limit_bytes=VMEM_LIMIT),
        name="rwkv",
    )(proj, proj, proj, proj, mu_r, mu_k, mu_v, mu_lo, w0, a0, k_k, k_a, r_k, ln_w, ln_b,
      ww2, aw2, gw2)


def _merge_mlp_kernel(x_ref, yg_ref, yr_ref, ga_ref, gb_ref, wg_ref, wr_ref, wo_ref,
                      g_ref, wu_ref, wd_ref, gf_ref, o_ref, *, final_norm):
    mixed = (_sigmoid(ga_ref[...]) * jnp.dot(yg_ref[...], wg_ref[...], preferred_element_type=F32)
             + _sigmoid(gb_ref[...]) * jnp.dot(yr_ref[...], wr_ref[...], preferred_element_type=F32))
    x = x_ref[...] + _mm(mixed, wo_ref[...])
    h = (x * lax.rsqrt(jnp.mean(x * x, axis=-1, keepdims=True) + NORM_EPS) * g_ref[...]).astype(BF16)
    up = jnp.dot(h, wu_ref[...], preferred_element_type=F32)
    act = jnp.square(jnp.maximum(up, 0.0)).astype(BF16)
    x = x + jnp.dot(act, wd_ref[...], preferred_element_type=F32)
    if final_norm:
        x = x * lax.rsqrt(jnp.mean(x * x, axis=-1, keepdims=True) + NORM_EPS) * gf_ref[...]
    o_ref[...] = x


def _merge_mlp(x2, yg, yr, proj, wg, wr, wo, g, wu, wd, gf, *, tm, final_norm):
    t = x2.shape[0]
    tok = lambda i: (i, 0)
    const = lambda i: (0, 0)
    resident = lambda shape: pl.BlockSpec(shape, const, pipeline_mode=pl.Buffered(1))
    return pl.pallas_call(
        functools.partial(_merge_mlp_kernel, final_norm=final_norm),
        grid=(t // tm,),
        in_specs=[
            pl.BlockSpec((tm, D_MODEL), tok),
            pl.BlockSpec((tm, D_MODEL), tok),
            pl.BlockSpec((tm, D_MODEL), tok),
            pl.BlockSpec((tm, D_MODEL), lambda i: (i, COL_GATE // D_MODEL)),
            pl.BlockSpec((tm, D_MODEL), lambda i: (i, COL_GATE // D_MODEL + 1)),
            resident((D_MODEL, D_MODEL)), resident((D_MODEL, D_MODEL)), resident((D_MODEL, D_MODEL)),
            resident((1, D_MODEL)),
            resident((D_MODEL, D_FF)),
            resident((D_FF, D_MODEL)),
            resident((1, D_MODEL)),
        ],
        out_specs=pl.BlockSpec((tm, D_MODEL), tok),
        out_shape=jax.ShapeDtypeStruct((t, D_MODEL), F32),
        compiler_params=pltpu.CompilerParams(
            dimension_semantics=("parallel",), vmem_limit_bytes=VMEM_LIMIT),
        name="merge_mlp",
    )(x2, yg, yr, proj, proj, wg, wr, wo, g, wu, wd, gf)


def _regroup_w_in(w):
    w = w.astype(BF16)
    gla_w = w[:, :GLA_WIDTH]
    rw = w[:, GLA_WIDTH:GLA_WIDTH + RWKV_WIDTH]
    gates = w[:, GLA_WIDTH + RWKV_WIDTH:]
    gla_main = GLA_WIDTH - GLA_GATE_RANK
    pad = jnp.zeros((w.shape[0], PROJ_WIDTH - COL_GAL - GLA_GATE_RANK), w.dtype)
    return jnp.concatenate([gates, gla_w[:, :gla_main], rw, gla_w[:, gla_main:], pad], axis=1)


def _row(v):
    return v.reshape(1, -1).astype(F32)


def _largest_divisor(n, candidates):
    return next(c for c in candidates if n % c == 0)


def _tile_plan(batch, seq):
    t = batch * seq
    return dict(
        tm=_largest_divisor(t, (TOKEN_TILE, CHUNK)),
        tn=PROJ_WIDTH // 2,
        lblk=_largest_divisor(seq, (SEQ_BLOCK, CHUNK)),
    )


def kernel(x, norm_mix, w_in, gla_a_w2, gla_a_b, gla_norm, rwkv_mu, rwkv_w0, rwkv_w_w2, rwkv_a0, rwkv_a_w2, rwkv_g_w2, rwkv_k_k, rwkv_k_a, rwkv_r_k, rwkv_ln_w, rwkv_ln_b, w_branch_gla, w_branch_rwkv, w_out, norm_mlp, w_up, w_down, norm_final):
    batch, seq, d = x.shape
    assert d == D_MODEL and seq % CHUNK == 0
    t = batch * seq
    depth = norm_mix.shape[0]
    plan = _tile_plan(batch, seq)
    tm, tn, lblk = plan["tm"], plan["tn"], plan["lblk"]
    x2 = x.reshape(t, d)
    for l in range(depth):
        wp = _regroup_w_in(w_in[l]).reshape(D_MODEL, PROJ_WIDTH // tn, tn).transpose(1, 0, 2)
        proj = _inproj(x2, _row(norm_mix[l]), wp, tm=tm, tn=tn)

        y_gla = _gla(proj, gla_a_w2[l].astype(BF16), _row(gla_a_b[l]), _row(gla_norm[l]),
                     batch=batch, seq=seq, lblk=lblk)

        mu = rwkv_mu[l]
        params = (_row(mu[0:RWKV_DIM]), _row(mu[RWKV_DIM:2 * RWKV_DIM]), _row(mu[2 * RWKV_DIM:3 * RWKV_DIM]),
                  _row(mu[3 * RWKV_DIM:]), _row(rwkv_w0[l]), _row(rwkv_a0[l]), _row(rwkv_k_k[l]),
                  _row(rwkv_k_a[l]), _row(rwkv_r_k[l]), _row(rwkv_ln_w[l]), _row(rwkv_ln_b[l]),
                  rwkv_w_w2[l].astype(BF16), rwkv_a_w2[l].astype(BF16), rwkv_g_w2[l].astype(BF16))
        y_rwkv = _rwkv(proj, params, batch=batch, seq=seq, lblk=lblk, n_pairs=RWKV_DIM // LANES)

        x2 = _merge_mlp(x2, y_gla, y_rwkv, proj, w_branch_gla[l].astype(BF16), w_branch_rwkv[l].astype(BF16),
                        w_out[l].astype(BF16), _row(norm_mlp[l]), w_up[l].astype(BF16),
                        w_down[l].astype(BF16), _row(norm_final), tm=tm, final_norm=(l == depth - 1))
    return x2.reshape(batch, seq, d)
```

```python
import functools

import jax
import jax.numpy as jnp
from jax import lax
from jax.experimental import pallas as pl
from jax.experimental.pallas import tpu as pltpu

F32 = jnp.float32
BF16 = jnp.bfloat16

D_MODEL = 1024
GLA_HEADS = 4
GLA_KEY = D_MODEL // 2
GLA_VAL = D_MODEL
GLA_DK = GLA_KEY // GLA_HEADS
GLA_DV = GLA_VAL // GLA_HEADS
GLA_GATE_RANK = 16
GLA_GATE_NORM = 16.0
GLA_NORM_EPS = 1e-5
RWKV_HEAD = 64
RWKV_DIM = D_MODEL
DECAY_LORA = 64
AAA_LORA = 64
GATE_LORA = 128
RWKV_GN_EPS = 64e-5
KK_NORM_FLOOR = 1e-12
DECAY_SCALE = 0.6065306597126334
D_FF = 4 * D_MODEL
NORM_EPS = 1e-6

GLA_WIDTH = 2 * GLA_KEY + 2 * GLA_VAL + GLA_GATE_RANK
RWKV_WIDTH = 3 * RWKV_DIM + DECAY_LORA + AAA_LORA + GATE_LORA

LANES = 128
MXU_COLS = 256
CHUNK = 64
TOKEN_TILE = 512
SEQ_BLOCK = 1024
LORA_WIDTH = DECAY_LORA + AAA_LORA + GATE_LORA
COL_GATE = 0
COL_GLA_QK = COL_GATE + 2 * D_MODEL
COL_GLA_V = COL_GLA_QK + 2 * GLA_KEY
COL_GLA_OG = COL_GLA_V + GLA_VAL
COL_R = COL_GLA_OG + GLA_VAL
COL_K = COL_R + RWKV_DIM
COL_V = COL_K + RWKV_DIM
COL_LORA = COL_V + RWKV_DIM
COL_GAL = COL_LORA + LORA_WIDTH
PROJ_WIDTH = -(-(COL_GAL + GLA_GATE_RANK) // (2 * MXU_COLS)) * (2 * MXU_COLS)

VMEM_CAPACITY = 64 * 1024 * 1024
VMEM_LIMIT = VMEM_CAPACITY * 7 // 8


def _mm(a, b):
    return jnp.dot(a.astype(BF16), b.astype(BF16), preferred_element_type=F32)


def _mm_nt(a, b):
    return lax.dot_general(a.astype(BF16), b.astype(BF16), (((1,), (1,)), ((), ())),
                           preferred_element_type=F32)


def _mm_tn(a, b):
    return lax.dot_general(a.astype(BF16), b.astype(BF16), (((0,), (0,)), ((), ())),
                           preferred_element_type=F32)


def _cumsum_rows(tri, x):
    tri = tri.astype(BF16)
    hi = x.astype(BF16)
    lo = (x - hi.astype(F32)).astype(BF16)
    return jnp.dot(jnp.concatenate([tri, tri], axis=1), jnp.concatenate([hi, lo], axis=0),
                   preferred_element_type=F32)


def _softplus(x):
    return jnp.maximum(x, 0.0) + jnp.log(1.0 + jnp.exp(-jnp.abs(x)))


def _sigmoid(x):
    return 1.0 / (1.0 + jnp.exp(-x))


def _iota(shape, dim):
    return lax.broadcasted_iota(jnp.int32, shape, dim)


def _chunk_rows(c):
    if isinstance(c, int):
        return pl.ds(c * CHUNK, CHUNK)
    return pl.ds(pl.multiple_of(c * CHUNK, CHUNK), CHUNK)


def _inproj_kernel(x_ref, g_ref, w_ref, o_ref, h_ref):
    @pl.when(pl.program_id(1) == 0)
    def _():
        x = x_ref[...]
        ms = jnp.mean(x * x, axis=-1, keepdims=True)
        h_ref[...] = (x * lax.rsqrt(ms + NORM_EPS) * g_ref[...]).astype(BF16)

    o_ref[...] = jnp.dot(h_ref[...], w_ref[pl.program_id(1)], preferred_element_type=F32)


def _inproj(x2, g, wp, *, tm, tn):
    t = x2.shape[0]
    return pl.pallas_call(
        _inproj_kernel,
        grid=(t // tm, PROJ_WIDTH // tn),
        in_specs=[
            pl.BlockSpec((tm, D_MODEL), lambda i, j: (i, 0)),
            pl.BlockSpec((1, D_MODEL), lambda i, j: (0, 0)),
            pl.BlockSpec((PROJ_WIDTH // tn, D_MODEL, tn), lambda i, j: (0, 0, 0),
                         pipeline_mode=pl.Buffered(1)),
        ],
        out_specs=pl.BlockSpec((tm, tn), lambda i, j: (i, j)),
        out_shape=jax.ShapeDtypeStruct((t, PROJ_WIDTH), F32),
        scratch_shapes=[pltpu.VMEM((tm, D_MODEL), BF16)],
        compiler_params=pltpu.CompilerParams(
            dimension_semantics=("parallel", "arbitrary"), vmem_limit_bytes=VMEM_LIMIT),
        name="inproj",
    )(x2, g, wp)


def _gla_kernel(qk_ref, v_ref, og_ref, gal_ref, w2_ref, ab_ref, gn_ref, y_ref, st_ref,
                qi_ref, ki_ref, qd_ref, kd_ref, dec_ref, o_ref, *, n_chunks, wide):
    @pl.when(pl.program_id(1) == 0)
    def _():
        st_ref[...] = jnp.zeros_like(st_ref)

    row = _iota((CHUNK, CHUNK), 0)
    col = _iota((CHUNK, CHUNK), 1)
    causal = row >= col
    tri = causal.astype(BF16)
    w2 = w2_ref[...]
    ab = ab_ref[...]
    gn = gn_ref[...]
    heads = range(GLA_HEADS)
    ks = [slice(h * GLA_DK, (h + 1) * GLA_DK) for h in heads]
    vs = [slice(h * GLA_DV, (h + 1) * GLA_DV) for h in heads]

    group = range(wide)
    n_groups = n_chunks // wide

    def rows_of(g, j):
        return _chunk_rows(g * wide + j)

    def stage1_gate(g):
        return g, [_mm(gal_ref[rows_of(g, j), 0:GLA_GATE_RANK], w2) for j in group]

    def stage1_cumsum(gate):
        g, lins = gate
        las = [-_softplus(-(lin + ab)) * (1.0 / GLA_GATE_NORM) for lin in lins]
        return g, [_cumsum_rows(tri, la) for la in las]

    def stage1_store(cs, slot):
        g, bs = cs
        for j in group:
            b = bs[j]
            rows = rows_of(g, j)
            b_mid = b[CHUNK // 2 - 1:CHUNK // 2, :]
            b_last = b[CHUNK - 1:CHUNK, :]
            q = qk_ref[rows, 0:GLA_KEY] * (GLA_DK ** -0.5)
            k = qk_ref[rows, GLA_KEY:2 * GLA_KEY]
            e_in = jnp.exp(b - b_mid)
            qi = q * e_in
            ki = k * (1.0 / e_in)
            qi_ref[slot, j] = qi.astype(BF16)
            ki_ref[slot, j] = ki.astype(BF16)
            qd_ref[slot, j] = (qi * jnp.exp(b_mid)).astype(BF16)
            kd_ref[slot, j] = (ki * jnp.exp(b_last - b_mid)).astype(BF16)
            dec_ref[slot, j] = jnp.exp(b_last)

    def stage2_front(g, slot):
        v = [[v_ref[rows_of(g, j), vs[h]].astype(BF16) for h in heads] for j in group]
        st = [st_ref[h] for h in heads]
        s = [[_mm_nt(qi_ref[slot, j, :, ks[h]], ki_ref[slot, j, :, ks[h]]) for h in heads] for j in group]
        upd = [[_mm_tn(v[j][h], kd_ref[slot, j, :, ks[h]]) for h in heads] for j in group]
        inter0 = [_mm_nt(qd_ref[slot, 0, :, ks[h]], st[h]) for h in heads]
        return v, st, s, upd, inter0

    def stage2_back(g, slot, front):
        v, st, s, upd, inter0 = front
        intra = [[_mm(jnp.where(causal, s[j][h], 0.0), v[j][h]) for h in heads] for j in group]
        inter = [inter0]
        for j in group:
            st = [st[h] * dec_ref[slot, j, :, ks[h]] + upd[j][h] for h in heads]
            if j + 1 < wide:
                inter.append([_mm_nt(qd_ref[slot, j + 1, :, ks[h]], st[h]) for h in heads])
        for h in heads:
            st_ref[h] = st[h]
        for j in group:
            for h in heads:
                o_ref[slot, j, :, vs[h]] = intra[j][h] + inter[j][h]

    def stage3(g, slot):
        for j in group:
            rows = rows_of(g, j)
            for h in heads:
                o = o_ref[slot, j, :, vs[h]]
                o = o * lax.rsqrt(jnp.mean(o * o, axis=-1, keepdims=True) + GLA_NORM_EPS) * gn
                og = og_ref[rows, vs[h]]
                y_ref[rows, vs[h]] = (o * (og * _sigmoid(og))).astype(BF16)

    def step(g):
        s1, s2, s3 = [0 <= g + k < n_groups if isinstance(g, int) else True for k in (1, 0, -1)]
        slot = g & 1
        if s3:
            stage3(g - 1, 1 - slot)
        if s1:
            gate = stage1_gate(g + 1)
        if s2:
            front = stage2_front(g, slot)
        if s1:
            sums = stage1_cumsum(gate)
        if s2:
            stage2_back(g, slot, front)
        if s1:
            stage1_store(sums, 1 - slot)

    step(-1)
    step(0)
    if n_groups > 2:
        def body(g, carry):
            step(g)
            return carry

        lax.fori_loop(1, n_groups - 1, body, 0)
    if n_groups > 1:
        step(n_groups - 1)
    step(n_groups)


def _gla(proj, w2, ab, gn, *, batch, seq, lblk):
    t = batch * seq
    nsb = seq // lblk
    row_map = lambda cb: (lambda b, i: (b * nsb + i, cb))
    const = lambda b, i: (0, 0)
    n_chunks = lblk // CHUNK
    wide = next(w for w in (4, 2, 1) if n_chunks % w == 0)
    hand = lambda rows, dtype: pltpu.VMEM((2, wide, rows, GLA_KEY), dtype)
    return pl.pallas_call(
        functools.partial(_gla_kernel, n_chunks=n_chunks, wide=wide),
        grid=(batch, nsb),
        in_specs=[
            pl.BlockSpec((lblk, 2 * GLA_KEY), row_map(COL_GLA_QK // (2 * GLA_KEY))),
            pl.BlockSpec((lblk, GLA_VAL), row_map(COL_GLA_V // GLA_VAL)),
            pl.BlockSpec((lblk, GLA_VAL), row_map(COL_GLA_OG // GLA_VAL)),
            pl.BlockSpec((lblk, LANES), row_map(COL_GAL // LANES)),
            pl.BlockSpec((GLA_GATE_RANK, GLA_KEY), const),
            pl.BlockSpec((1, GLA_KEY), const),
            pl.BlockSpec((1, GLA_DV), const),
        ],
        out_specs=pl.BlockSpec((lblk, GLA_VAL), lambda b, i: (b * nsb + i, 0)),
        out_shape=jax.ShapeDtypeStruct((t, GLA_VAL), BF16),
        scratch_shapes=[
            pltpu.VMEM((GLA_HEADS, GLA_DV, GLA_DK), F32),
            hand(CHUNK, BF16), hand(CHUNK, BF16), hand(CHUNK, BF16), hand(CHUNK, BF16),
            hand(1, F32),
            pltpu.VMEM((2, wide, CHUNK, GLA_VAL), F32),
        ],
        compiler_params=pltpu.CompilerParams(
            dimension_semantics=("parallel", "arbitrary"), vmem_limit_bytes=VMEM_LIMIT),
        name="gla",
    )(proj, proj, proj, proj, w2, ab, gn)


def _pair_block_diag(y, lo):
    zero = jnp.zeros_like(y)
    return jnp.concatenate([jnp.where(lo, y, zero), jnp.where(lo, zero, y)], axis=0)


def _unit_lower_inverse(ns, masks, after_stage=None):
    after_stage = after_stage or {}
    stage = [0]

    def stage_done():
        hook = after_stage.get(stage[0])
        if hook is not None:
            hook()
        stage[0] += 1

    eye, blk16, off32, off64, lo = masks
    nd = [jnp.where(blk16, n, 0.0) for n in ns]
    x = [eye + d for d in nd]
    nd = [d.astype(BF16) for d in nd]
    m = [_mm(d, _pair_block_diag(d, lo)).astype(BF16) for d in nd]
    stage_done()
    c = ns[0].shape[0]
    for step in range(3):
        mb = [_pair_block_diag(mi, lo) for mi in m]
        if step < 2:
            xm = [_mm(jnp.concatenate([xi.astype(BF16), mi], axis=0), mbi) for xi, mi, mbi in zip(x, m, mb)]
            m = [r[c:].astype(BF16) for r in xm]
            xm = [r[:c] for r in xm]
        else:
            xm = [_mm(xi, mbi) for xi, mbi in zip(x, mb)]
        x = [xi + xmi for xi, xmi in zip(x, xm)]
        stage_done()
    for off in (off32, off64):
        xb = [xi.astype(BF16) for xi in x]
        xn = [_mm(xi, _pair_block_diag(jnp.where(off, n, 0.0).astype(BF16), lo)) for xi, n in zip(xb, ns)]
        stage_done()
        xnx = [_mm(a, _pair_block_diag(xi, lo)) for a, xi in zip(xn, xb)]
        x = [xi + b for xi, b in zip(x, xnx)]
        stage_done()
    return x


def _rwkv_kernel(r_ref, k_ref, v_ref, lo_ref, mu_r_ref, mu_k_ref, mu_v_ref, mu_lo_ref,
                 w0_ref, a0_ref, kk_ref, ka_ref, rk_ref, lnw_ref, lnb_ref,
                 ww2_ref, aw2_ref, gw2_ref, y_ref,
                 pr_ref, pk_ref, pv_ref, plo_ref, st_ref,
                 n_ref, prm_ref, ahl_ref, bk2_ref, rkv_ref, akv_ref, v2_ref, bonv_ref, gate_ref, elast_ref,
                 t_ref,
                 *, n_chunks, n_pairs):
    @pl.when(pl.program_id(2) == 0)
    def _():
        pr_ref[...] = jnp.zeros_like(pr_ref)
        pk_ref[...] = jnp.zeros_like(pk_ref)
        pv_ref[...] = jnp.zeros_like(pv_ref)
        plo_ref[...] = jnp.zeros_like(plo_ref)
        st_ref[...] = jnp.zeros_like(st_ref)

    width = n_pairs * LANES
    tri = (_iota((CHUNK, CHUNK), 0) >= _iota((CHUNK, CHUNK), 1)).astype(BF16)

    trow = _iota((CHUNK, LANES), 0)
    tcol = _iota((CHUNK, LANES), 1) & (RWKV_HEAD - 1)
    lo = _iota((CHUNK, LANES), 1) < RWKV_HEAD
    strict = trow > tcol
    incl = trow >= tcol
    blk16 = (trow // 16) == (tcol // 16)
    blk32 = (trow // 32) == (tcol // 32)
    inv_masks = ((trow == tcol).astype(F32), blk16, blk32 & (~blk16), ~blk32, lo)
    srow = _iota((LANES, LANES), 0)
    scol = _iota((LANES, LANES), 1)
    same = (srow < RWKV_HEAD) == (scol < RWKV_HEAD)
    first_w = _iota((CHUNK, width), 0) == 0
    first_lo = _iota((CHUNK, LORA_WIDTH), 0) == 0
    pairs = range(n_pairs)
    ls = [slice(p * LANES, (p + 1) * LANES) for p in pairs]

    def seg_sum(x):
        s0 = jnp.sum(jnp.where(lo, x, 0.0), axis=-1, keepdims=True)
        s1 = jnp.sum(jnp.where(lo, 0.0, x), axis=-1, keepdims=True)
        return jnp.where(lo, s0, s1)

    def stack(a, b):
        return jnp.concatenate([a, b], axis=0)

    def shifted(z, prev_ref, first):
        zs = jnp.where(first, prev_ref[...], pltpu.roll(z, 1, axis=0))
        prev_ref[...] = z[CHUNK - 1:CHUNK, :]
        return zs

    def stage1_lora(c):
        rows = _chunk_rows(c)
        zl = lo_ref[rows, :]
        lora = zl + (shifted(zl, plo_ref, first_lo) - zl) * mu_lo_ref[...]
        wl = lora[:, 0:DECAY_LORA]
        al = lora[:, DECAY_LORA:DECAY_LORA + AAA_LORA]
        gl = lora[:, DECAY_LORA + AAA_LORA:LORA_WIDTH]
        return (c, _mm(jnp.tanh(wl), ww2_ref[...]), _mm(al, aw2_ref[...]),
                _mm(_sigmoid(gl), gw2_ref[...]))

    def stage1_decay(lora_out):
        c, w_lin, a_lin, gate_all = lora_out
        logw_all = (-DECAY_SCALE) * _sigmoid(w0_ref[...] + w_lin)
        return c, logw_all, a_lin, gate_all, _cumsum_rows(tri, logw_all)

    def stage1_back(front, slot):
        c, logw_all, a_lin, gate_all, lw_all = front
        rows = _chunk_rows(c)
        zr = r_ref[rows, :]
        zk = k_ref[rows, :]
        zv = v_ref[rows, :]
        r_all = zr + (shifted(zr, pr_ref, first_w) - zr) * mu_r_ref[...]
        k_all = zk + (shifted(zk, pk_ref, first_w) - zk) * mu_k_ref[...]
        v_all = zv + (shifted(zv, pv_ref, first_w) - zv) * mu_v_ref[...]
        a_all = _sigmoid(a0_ref[...] + a_lin)

        r = [r_all[:, s] for s in ls]
        k = [k_all[:, s] for s in ls]
        v = [v_all[:, s] for s in ls]
        a = [a_all[:, s] for s in ls]
        lw = [lw_all[:, s] for s in ls]
        kk = [k[p] * kk_ref[:, ls[p]] for p in pairs]
        kkn = [kk[p] * lax.rsqrt(jnp.maximum(seg_sum(kk[p] * kk[p]), KK_NORM_FLOOR ** 2)) for p in pairs]
        kmod = [k[p] * (1.0 + (a[p] - 1.0) * ka_ref[:, ls[p]]) for p in pairs]
        bvec = [kkn[p] * a[p] for p in pairs]
        mid = [x[CHUNK // 2 - 1:CHUNK // 2, :] for x in lw]
        last = [x[CHUNK - 1:CHUNK, :] for x in lw]
        zero = jnp.zeros((CHUNK, LANES), BF16)
        e_in = [jnp.exp(lw[p] - mid[p]) for p in pairs]
        e_ng = [1.0 / e_in[p] for p in pairs]
        ag_f = [-kkn[p] * (e_in[p] * jnp.exp(-logw_all[:, ls[p]])) for p in pairs]
        rg_f = [r[p] * e_in[p] for p in pairs]
        bg_f = [bvec[p] * e_ng[p] for p in pairs]
        kg_f = [kmod[p] * e_ng[p] for p in pairs]
        ag = [x.astype(BF16) for x in ag_f]
        rg = [x.astype(BF16) for x in rg_f]
        v16 = [v[p].astype(BF16) for p in pairs]
        pm = [_mm_nt(jnp.concatenate([jnp.where(lo, ag[p], zero), jnp.where(lo, rg[p], zero),
                                      jnp.where(lo, zero, ag[p]), jnp.where(lo, zero, rg[p])], axis=0),
                     stack(bg_f[p].astype(BF16), kg_f[p].astype(BF16))) for p in pairs]
        pm0 = [x[0:2 * CHUNK] for x in pm]
        pm1 = [pltpu.roll(x[2 * CHUNK:4 * CHUNK], RWKV_HEAD, axis=1) for x in pm]
        ak = [jnp.where(strict, jnp.where(lo, pm1[p][0:CHUNK], pm0[p][0:CHUNK]), 0.0) for p in pairs]
        rk = [jnp.where(incl, jnp.where(lo, pm1[p][CHUNK:], pm0[p][CHUNK:]), 0.0) for p in pairs]
        vx = [stack(jnp.where(lo, zero, v16[p]), jnp.where(lo, v16[p], zero)) for p in pairs]
        akrk = [_mm(stack(ak[p], rk[p]), vx[p]) for p in pairs]
        for p in pairs:
            e_mid = jnp.exp(mid[p])
            e_tail = jnp.exp(last[p] - mid[p])
            n_ref[slot, p] = jnp.where(strict, jnp.where(lo, pm0[p][0:CHUNK], pm1[p][0:CHUNK]), 0.0)
            rb = jnp.where(incl, jnp.where(lo, pm0[p][CHUNK:], pm1[p][CHUNK:]), 0.0)
            prm_ref[slot, p] = rb.astype(BF16)
            akv_ref[slot, p] = akrk[p][0:CHUNK]
            rkv_ref[slot, p] = akrk[p][CHUNK:]
            ahl_ref[slot, p] = stack((ag_f[p] * e_mid).astype(BF16), (rg_f[p] * e_mid).astype(BF16))
            bk2_ref[slot, p] = stack((bg_f[p] * e_tail).astype(BF16), (kg_f[p] * e_tail).astype(BF16))
            v2_ref[slot, p] = v16[p]
            bonv_ref[slot, p] = seg_sum(r[p] * kmod[p] * rk_ref[:, ls[p]]) * v[p]
            gate_ref[slot, p] = gate_all[:, ls[p]]
            elast_ref[slot, p] = jnp.exp(last[p])

    def stage2(slot, tslot, after_stage=None):
        tinv = _unit_lower_inverse([n_ref[slot, p] for p in pairs], inv_masks, after_stage)
        for p in pairs:
            t_ref[tslot, p] = tinv[p].astype(BF16)

    def stage3_state(slot):
        st = [st_ref[p] for p in pairs]
        ah = [_mm_nt(ahl_ref[slot, p], st[p]) for p in pairs]
        return st, ah

    def stage3_solve(slot, tslot, front):
        st, ah = front
        w = [(ah[p][0:CHUNK] + akv_ref[slot, p]).astype(BF16) for p in pairs]
        return [_mm(t_ref[tslot, p], _pair_block_diag(w[p], lo)).astype(BF16) for p in pairs]

    def stage3_out(c, slot, front, u):
        st, ah = front
        rows = _chunk_rows(c)
        ys = [_mm(prm_ref[slot, p], _pair_block_diag(u[p], lo)) + rkv_ref[slot, p] for p in pairs]
        dh = [_mm_tn(stack(u[p], v2_ref[slot, p]), bk2_ref[slot, p]) for p in pairs]
        for p in pairs:
            st_ref[p] = st[p] * elast_ref[slot, p] + jnp.where(same, dh[p], 0.0)
            y = ah[p][CHUNK:] + ys[p]
            mean = seg_sum(y) * (1.0 / RWKV_HEAD)
            d = y - mean
            var = seg_sum(d * d) * (1.0 / RWKV_HEAD)
            yn = d * lax.rsqrt(var + RWKV_GN_EPS) * lnw_ref[:, ls[p]] + lnb_ref[:, ls[p]]
            y_ref[rows, ls[p]] = ((yn + bonv_ref[slot, p]) * gate_ref[slot, p]).astype(BF16)

    def step(c):
        s1, s2, s3 = [0 <= c + k < n_chunks if isinstance(c, int) else True for k in (2, 1, 0)]
        slot3, slot2, slot1 = c % 3, (c + 1) % 3, (c + 2) % 3
        tslot3, tslot2 = c & 1, (c + 1) & 1
        held = {}

        def after0():
            if s1:
                held["lora"] = stage1_lora(c + 2)
            if s3:
                held["u"] = stage3_solve(slot3, tslot3, held["state"])

        def after1():
            if s3:
                stage3_out(c, slot3, held["state"], held["u"])

        def after2():
            if s1:
                held["front"] = stage1_decay(held["lora"])

        if s3:
            held["state"] = stage3_state(slot3)
        if s2:
            stage2(slot2, tslot2, {0: after0, 1: after1, 2: after2})
        else:
            after0(), after1(), after2()
        if s1:
            stage1_back(held["front"], slot1)

    n_steady = max(n_chunks - 2, 0)
    for c in range(-2, 0):
        step(c)
    if n_steady > 0:
        def body(c, carry):
            step(c)
            return carry

        lax.fori_loop(0, n_steady, body, 0)
    for c in range(n_steady, n_chunks):
        step(c)


def _rwkv(proj, params, *, batch, seq, lblk, n_pairs):
    (mu_r, mu_k, mu_v, mu_lo, w0, a0, k_k, k_a, r_k, ln_w, ln_b, ww2, aw2, gw2) = params
    t = batch * seq
    nsb = seq // lblk
    width = n_pairs * LANES
    ngrp = RWKV_DIM // width
    row_map = lambda c0: (lambda b, g, i: (b * nsb + i, c0 // width + g))
    grp = lambda b, g, i: (0, g)
    const = lambda b, g, i: (0, 0)
    vec = pl.BlockSpec((1, width), grp)
    rec = lambda slots, rows, cols, dtype: pltpu.VMEM((slots, n_pairs, rows, cols), dtype)
    return pl.pallas_call(
        functools.partial(_rwkv_kernel, n_chunks=lblk // CHUNK, n_pairs=n_pairs),
        grid=(batch, ngrp, nsb),
        in_specs=[
            pl.BlockSpec((lblk, width), row_map(COL_R)),
            pl.BlockSpec((lblk, width), row_map(COL_K)),
            pl.BlockSpec((lblk, width), row_map(COL_V)),
            pl.BlockSpec((lblk, LORA_WIDTH), lambda b, g, i: (b * nsb + i, COL_LORA // LORA_WIDTH)),
            vec, vec, vec,
            pl.BlockSpec((1, LORA_WIDTH), const),
            vec, vec, vec, vec, vec, vec, vec,
            pl.BlockSpec((DECAY_LORA, width), grp),
            pl.BlockSpec((AAA_LORA, width), grp),
            pl.BlockSpec((GATE_LORA, width), grp),
        ],
        out_specs=pl.BlockSpec((lblk, width), lambda b, g, i: (b * nsb + i, g)),
        out_shape=jax.ShapeDtypeStruct((t, RWKV_DIM), BF16),
        scratch_shapes=[
            pltpu.VMEM((1, width), F32), pltpu.VMEM((1, width), F32), pltpu.VMEM((1, width), F32),
            pltpu.VMEM((1, LORA_WIDTH), F32),
            pltpu.VMEM((n_pairs, LANES, LANES), F32),
            rec(3, CHUNK, LANES, F32),
            rec(3, CHUNK, LANES, BF16),
            rec(3, LANES, LANES, BF16),
            rec(3, LANES, LANES, BF16),
            rec(3, CHUNK, LANES, F32),
            rec(3, CHUNK, LANES, F32),
            rec(3, CHUNK, LANES, BF16),
            rec(3, CHUNK, LANES, F32),
            rec(3, CHUNK, LANES, F32),
            rec(3, 1, LANES, F32),
            rec(2, CHUNK, LANES, BF16),
        ],
        compiler_params=pltpu.CompilerParams(
            dimension_semantics=("parallel", "parallel", "arbitrary"), vmem_limit_bytes=VMEM_LIMIT),
        name="rwkv",
    )(proj, proj, proj, proj, mu_r, mu_k, mu_v, mu_lo, w0, a0, k_k, k_a, r_k, ln_w, ln_b,
      ww2, aw2, gw2)


def _merge_mlp_kernel(x_ref, yg_ref, yr_ref, ga_ref, gb_ref, wg_ref, wr_ref, wo_ref,
                      g_ref, wu_ref, wd_ref, gf_ref, o_ref, *, final_norm):
    mixed = (_sigmoid(ga_ref[...]) * jnp.dot(yg_ref[...], wg_ref[...], preferred_element_type=F32)
             + _sigmoid(gb_ref[...]) * jnp.dot(yr_ref[...], wr_ref[...], preferred_element_type=F32))
    x = x_ref[...] + _mm(mixed, wo_ref[...])
    h = (x * lax.rsqrt(jnp.mean(x * x, axis=-1, keepdims=True) + NORM_EPS) * g_ref[...]).astype(BF16)
    for f in range(0, D_FF, D_MODEL):
        up = jnp.dot(h, wu_ref[:, f:f + D_MODEL], preferred_element_type=F32)
        act = jnp.square(jnp.maximum(up, 0.0)).astype(BF16)
        x = x + jnp.dot(act, wd_ref[f:f + D_MODEL, :], preferred_element_type=F32)
    if final_norm:
        x = x * lax.rsqrt(jnp.mean(x * x, axis=-1, keepdims=True) + NORM_EPS) * gf_ref[...]
    o_ref[...] = x


def _merge_mlp(x2, yg, yr, proj, wg, wr, wo, g, wu, wd, gf, *, tm, final_norm):
    t = x2.shape[0]
    tok = lambda i: (i, 0)
    const = lambda i: (0, 0)
    resident = lambda shape: pl.BlockSpec(shape, const, pipeline_mode=pl.Buffered(1))
    return pl.pallas_call(
        functools.partial(_merge_mlp_kernel, final_norm=final_norm),
        grid=(t // tm,),
        in_specs=[
            pl.BlockSpec((tm, D_MODEL), tok),
            pl.BlockSpec((tm, D_MODEL), tok),
            pl.BlockSpec((tm, D_MODEL), tok),
            pl.BlockSpec((tm, D_MODEL), lambda i: (i, COL_GATE // D_MODEL)),
            pl.BlockSpec((tm, D_MODEL), lambda i: (i, COL_GATE // D_MODEL + 1)),
            resident((D_MODEL, D_MODEL)), resident((D_MODEL, D_MODEL)), resident((D_MODEL, D_MODEL)),
            resident((1, D_MODEL)),
            resident((D_MODEL, D_FF)),
            resident((D_FF, D_MODEL)),
            resident((1, D_MODEL)),
        ],
        out_specs=pl.BlockSpec((tm, D_MODEL), tok),
        out_shape=jax.ShapeDtypeStruct((t, D_MODEL), F32),
        compiler_params=pltpu.CompilerParams(
            dimension_semantics=("parallel",), vmem_limit_bytes=VMEM_LIMIT),
        name="merge_mlp",
    )(x2, yg, yr, proj, proj, wg, wr, wo, g, wu, wd, gf)


def _regroup_w_in(w):
    w = w.astype(BF16)
    gla_w = w[:, :GLA_WIDTH]
    rw = w[:, GLA_WIDTH:GLA_WIDTH + RWKV_WIDTH]
    gates = w[:, GLA_WIDTH + RWKV_WIDTH:]
    gla_main = GLA_WIDTH - GLA_GATE_RANK
    pad = jnp.zeros((w.shape[0], PROJ_WIDTH - COL_GAL - GLA_GATE_RANK), w.dtype)
    return jnp.concatenate([gates, gla_w[:, :gla_main], rw, gla_w[:, gla_main:], pad], axis=1)


def _row(v):
    return v.reshape(1, -1).astype(F32)


def _largest_divisor(n, candidates):
    return next(c for c in candidates if n % c == 0)


def _tile_plan(batch, seq):
    t = batch * seq
    return dict(
        tm=_largest_divisor(t, (TOKEN_TILE, CHUNK)),
        tn=PROJ_WIDTH // 2,
        lblk=_largest_divisor(seq, (SEQ_BLOCK, CHUNK)),
    )


def kernel(x, norm_mix, w_in, gla_a_w2, gla_a_b, gla_norm, rwkv_mu, rwkv_w0, rwkv_w_w2, rwkv_a0, rwkv_a_w2, rwkv_g_w2, rwkv_k_k, rwkv_k_a, rwkv_r_k, rwkv_ln_w, rwkv_ln_b, w_branch_gla, w_branch_rwkv, w_out, norm_mlp, w_up, w_down, norm_final):
    batch, seq, d = x.shape
    assert d == D_MODEL and seq % CHUNK == 0
    t = batch * seq
    depth = norm_mix.shape[0]
    plan = _tile_plan(batch, seq)
    tm, tn, lblk = plan["tm"], plan["tn"], plan["lblk"]
    x2 = x.reshape(t, d)
    for l in range(depth):
        wp = _regroup_w_in(w_in[l]).reshape(D_MODEL, PROJ_WIDTH // tn, tn).transpose(1, 0, 2)
        proj = _inproj(x2, _row(norm_mix[l]), wp, tm=tm, tn=tn)

        y_gla = _gla(proj, gla_a_w2[l].astype(BF16), _row(gla_a_b[l]), _row(gla_norm[l]),
                     batch=batch, seq=seq, lblk=lblk)

        mu = rwkv_mu[l]
        params = (_row(mu[0:RWKV_DIM]), _row(mu[RWKV_DIM:2 * RWKV_DIM]), _row(mu[2 * RWKV_DIM:3 * RWKV_DIM]),
                  _row(mu[3 * RWKV_DIM:]), _row(rwkv_w0[l]), _row(rwkv_a0[l]), _row(rwkv_k_k[l]),
                  _row(rwkv_k_a[l]), _row(rwkv_r_k[l]), _row(rwkv_ln_w[l]), _row(rwkv_ln_b[l]),
                  rwkv_w_w2[l].astype(BF16), rwkv_a_w2[l].astype(BF16), rwkv_g_w2[l].astype(BF16))
        y_rwkv = _rwkv(proj, params, batch=batch, seq=seq, lblk=lblk, n_pairs=RWKV_DIM // LANES)

        x2 = _merge_mlp(x2, y_gla, y_rwkv, proj, w_branch_gla[l].astype(BF16), w_branch_rwkv[l].astype(BF16),
                        w_out[l].astype(BF16), _row(norm_mlp[l]), w_up[l].astype(BF16),
                        w_down[l].astype(BF16), _row(norm_final), tm=tm, final_norm=(l == depth - 1))
    return x2.reshape(batch, seq, d)
```
